```python
import math
import jax, jax.numpy as jnp
from jax import lax
import numpy as np

D_MODEL = 2048
BATCH = 16
SEQ = 256
DEPTH = 1
DEC_BATCH = 4
DEC_SEQ = 4096
PAST_LEN = 256

GRID_W = 64
D_RNN = 1024
RNN_BLOCKS = 8
RNN_BS = D_RNN // RNN_BLOCKS
CONV_W = 4
CONV_LEFT = 2
LRU_C = 8.0
N_HEADS = 8
HD = 64
VD = 2 * HD
QK_W = N_HEADS * 2 * HD
ATT_W = N_HEADS * VD
Q_BLOCK = 128
ROPE_BASE = 10000.0
ROPE_AXIS_DIM = HD // 2
N_EXPERTS = 16
D_EXPERT = 1024
EC_FACTOR = 2
EPS = 1e-6
IN_COLS = 2 * D_RNN + 2 * QK_W + ATT_W + 2 * D_MODEL
SPLITS = (D_RNN, 2 * D_RNN, 2 * D_RNN + QK_W, 2 * D_RNN + 2 * QK_W, 2 * D_RNN + 2 * QK_W + ATT_W)

kernel_name = 'hybrid_diffusion_rglru_diffattn_ec'


def rmsnorm(x, g):
    xf = x.astype(jnp.float32)
    r = xf * lax.rsqrt(jnp.mean(xf * xf, axis=-1, keepdims=True) + EPS)
    return (r * g.astype(jnp.float32)).astype(x.dtype)


def modulation(cvec, w_mod, b_mod):
    m = jax.nn.silu(cvec) @ w_mod + b_mod
    return m[..., None, :]


def centred_conv(x, w, b):
    n = x.shape[1]
    xp = jnp.pad(x, ((0, 0), (CONV_LEFT, CONV_W - 1 - CONV_LEFT), (0, 0)))
    out = xp[:, 0:n] * w[0]
    for tap in range(1, CONV_W):
        out = out + xp[:, tap:tap + n] * w[tap]
    return out + b


def block_diag(x, w, b):
    B, n, C = x.shape
    xb = x.reshape(B, n, RNN_BLOCKS, RNN_BS)
    y = jnp.einsum('bnki,kij->bnkj', xb, w.astype(x.dtype))
    return y.reshape(B, n, C) + b.astype(x.dtype)


def _lin_comb(left, right):
    a1, b1 = left
    a2, b2 = right
    return a1 * a2, a2 * b1 + b2


def rglru_bidir(xc, wr, br, wi, bi, lam, h0):
    xf = xc.astype(jnp.float32)
    ys = []
    finals = []
    for d, rev in ((0, False), (1, True)):
        r = jax.nn.sigmoid(block_diag(xf, wr[d], br[d]))
        i = jax.nn.sigmoid(block_diag(xf, wi[d], bi[d]))
        log_a = -LRU_C * r * jax.nn.softplus(-lam[d].astype(jnp.float32))
        a = jnp.exp(log_a)
        bterm = jnp.sqrt(-jnp.expm1(2.0 * log_a)) * (i * xf)
        ca, cb = lax.associative_scan(_lin_comb, (a, bterm), reverse=rev, axis=1)
        h = ca * h0[:, d, None, :].astype(jnp.float32) + cb
        ys.append(h)
        finals.append(h[:, 0] if rev else h[:, -1])
    y = ys[0] + ys[1]
    return y.astype(xc.dtype), jnp.stack(finals, axis=1)


def axial_rope(n):
    rows = n // GRID_W
    row = jnp.repeat(jnp.arange(rows), GRID_W).astype(jnp.float32)
    col = jnp.tile(jnp.arange(GRID_W), rows).astype(jnp.float32)
    inv = ROPE_BASE ** (-jnp.arange(0, ROPE_AXIS_DIM, 2, dtype=jnp.float32) / ROPE_AXIS_DIM)
    ang = jnp.concatenate([row[:, None] * inv, col[:, None] * inv], axis=-1)
    return jnp.cos(ang), jnp.sin(ang)


def apply_rope(x, cos, sin):
    cos = cos[None, :, None, None, :].astype(x.dtype)
    sin = sin[None, :, None, None, :].astype(x.dtype)
    x1 = x[..., :HD // 2]
    x2 = x[..., HD // 2:]
    return jnp.concatenate([x1 * cos - x2 * sin, x2 * cos + x1 * sin], axis=-1)


def diff_attn(q, k, v, lam):
    s = jnp.einsum('bqhcd,bkhcd->bhcqk', q, k).astype(jnp.float32) * (HD ** -0.5)
    p = jax.nn.softmax(s, axis=-1)
    pd = p[:, :, 0] - lam * p[:, :, 1]
    return jnp.einsum('bhqk,bkhd->bqhd', pd.astype(v.dtype), v)


def blocked_diff_attn(q, k, v, lam):
    B, n = q.shape[0], q.shape[1]
    nb = n // Q_BLOCK
    qb = q.reshape(B, nb, Q_BLOCK, N_HEADS, 2, HD).swapaxes(0, 1)
    ob = lax.map(lambda qi: diff_attn(qi, k, v, lam), qb)
    return ob.swapaxes(0, 1).reshape(B, n, N_HEADS, VD)


def token_mixer(h, lp, layer_idx, h0, ctx_k, ctx_v, rope):
    B, n, _ = h.shape
    proj = h @ lp['w_in']
    xr, gr, q, k, v, gates = jnp.split(proj, SPLITS, axis=-1)
    xc = centred_conv(xr, lp['conv_w'], lp['conv_b'])
    y, h_fin = rglru_bidir(xc, lp['lru_wr'], lp['lru_br'], lp['lru_wi'], lp['lru_bi'], lp['lru_lam'], h0)
    out_a = (y * jax.nn.gelu(gr)) @ lp['w_a_out']
    q = q.reshape(B, n, N_HEADS, 2, HD)
    k = k.reshape(B, n, N_HEADS, 2, HD)
    v = v.reshape(B, n, N_HEADS, VD)
    lam_init = 0.8 - 0.6 * math.exp(-0.3 * layer_idx)
    lam = (jnp.exp(jnp.sum(lp['lam_q1'].astype(jnp.float32) * lp['lam_k1'].astype(jnp.float32)))
           - jnp.exp(jnp.sum(lp['lam_q2'].astype(jnp.float32) * lp['lam_k2'].astype(jnp.float32)))
           + lam_init)
    if ctx_k is None:
        o = diff_attn(q, k, v, lam)
    else:
        q = apply_rope(q, *rope)
        k = apply_rope(k, *rope)
        k_all = jnp.concatenate([ctx_k.astype(k.dtype), k], axis=1)
        v_all = jnp.concatenate([ctx_v.astype(v.dtype), v], axis=1)
        o = blocked_diff_attn(q, k_all, v_all, lam)
    o = rmsnorm(o, lp['g_sub']) * (1.0 - lam_init)
    out_b = o.reshape(B, n, ATT_W) @ lp['w_b_out']
    ga, gb = jnp.split(jax.nn.sigmoid(gates), 2, axis=-1)
    out = (ga * out_a + gb * out_b) @ lp['w_o']
    return out, k.reshape(B, n, N_HEADS, 2 * HD), v, h_fin


def expert_choice_ffn(h, w_router, w_gate, w_up, w_down):
    B, n, D = h.shape
    cap = EC_FACTOR * n // N_EXPERTS
    aff = jax.nn.softmax((h @ w_router).astype(jnp.float32), axis=-1)
    vals, idx = lax.top_k(aff.swapaxes(1, 2), cap)
    xs = jax.vmap(lambda hb, ib: hb[ib])(h, idx)
    hid = jax.nn.silu(jnp.einsum('becd,edf->becf', xs, w_gate)) * jnp.einsum('becd,edf->becf', xs, w_up)
    ye = jnp.einsum('becf,efd->becd', hid, w_down) * vals[..., None].astype(h.dtype)
    return jax.vmap(lambda ib, yb: jnp.zeros((n, D), yb.dtype).at[ib.reshape(-1)].add(yb.reshape(-1, D)))(idx, ye)


def trunk_layer(x, mod, lp, layer_idx, h0, ctx_k, ctx_v, rope):
    sh1, sc1, g1, sh2, sc2, g2 = jnp.split(mod, 6, axis=-1)
    h = rmsnorm(x, lp['g_pre1']) * (1 + sc1) + sh1
    t, k, v, h_fin = token_mixer(h, lp, layer_idx, h0, ctx_k, ctx_v, rope)
    x = x + g1 * rmsnorm(t, lp['g_post1'])
    h = rmsnorm(x, lp['g_pre2']) * (1 + sc2) + sh2
    f = expert_choice_ffn(h, lp['w_router'], lp['w_e_gate'], lp['w_e_up'], lp['w_e_down'])
    x = x + g2 * rmsnorm(f, lp['g_post2'])
    return x, k, v, h_fin


def setup_inputs(seed: int = 0) -> dict:
    key = jax.random.key(seed)
    ks = jax.random.split(key, 40)
    f32 = jnp.float32

    def nrm(i, shape, scale):
        return jax.random.normal(ks[i], shape, f32) * scale

    u = jax.random.uniform(ks[39], (DEPTH, 2, D_RNN), f32, 0.9, 0.999)
    return {
        'x_prompt': nrm(0, (BATCH, SEQ, D_MODEL), 1.0),
        'x_sample': nrm(1, (DEC_BATCH, DEC_SEQ, D_MODEL), 1.0),
        'cache_k': nrm(2, (DEC_BATCH, DEPTH, PAST_LEN, N_HEADS, 2 * HD), 1.0),
        'cache_v': nrm(3, (DEC_BATCH, DEPTH, PAST_LEN, N_HEADS, VD), 1.0),
        'state_rnn': nrm(4, (DEC_BATCH, DEPTH, 2, D_RNN), 0.5),
        'c': nrm(5, (DEC_BATCH, D_MODEL), 1.0),
        'c_ctx': nrm(6, (D_MODEL,), 1.0),
        'w_mod': nrm(7, (DEPTH, D_MODEL, 6 * D_MODEL), 0.5 * D_MODEL ** -0.5),
        'b_mod': nrm(8, (DEPTH, 6 * D_MODEL), 0.01),
        'g_pre1': 1.0 + nrm(9, (DEPTH, D_MODEL), 0.05),
        'g_post1': 1.0 + nrm(10, (DEPTH, D_MODEL), 0.05),
        'g_pre2': 1.0 + nrm(11, (DEPTH, D_MODEL), 0.05),
        'g_post2': 1.0 + nrm(12, (DEPTH, D_MODEL), 0.05),
        'w_in': nrm(13, (DEPTH, D_MODEL, IN_COLS), D_MODEL ** -0.5),
        'conv_w': nrm(14, (DEPTH, CONV_W, D_RNN), 0.5),
        'conv_b': nrm(15, (DEPTH, D_RNN), 0.01),
        'lru_wr': nrm(16, (DEPTH, 2, RNN_BLOCKS, RNN_BS, RNN_BS), RNN_BS ** -0.5),
        'lru_br': nrm(17, (DEPTH, 2, D_RNN), 0.01),
        'lru_wi': nrm(18, (DEPTH, 2, RNN_BLOCKS, RNN_BS, RNN_BS), RNN_BS ** -0.5),
        'lru_bi': nrm(19, (DEPTH, 2, D_RNN), 0.01),
        'lru_lam': jnp.log(u) - jnp.log1p(-u),
        'lam_q1': nrm(20, (DEPTH, HD), 0.1),
        'lam_k1': nrm(21, (DEPTH, HD), 0.1),
        'lam_q2': nrm(22, (DEPTH, HD), 0.1),
        'lam_k2': nrm(23, (DEPTH, HD), 0.1),
        'g_sub': 1.0 + nrm(24, (DEPTH, VD), 0.05),
        'w_a_out': nrm(25, (DEPTH, D_RNN, D_MODEL), D_RNN ** -0.5),
        'w_b_out': nrm(26, (DEPTH, ATT_W, D_MODEL), ATT_W ** -0.5),
        'w_o': nrm(27, (DEPTH, D_MODEL, D_MODEL), D_MODEL ** -0.5),
        'w_router': nrm(28, (DEPTH, D_MODEL, N_EXPERTS), D_MODEL ** -0.5),
        'w_e_gate': nrm(29, (DEPTH, N_EXPERTS, D_MODEL, D_EXPERT), D_MODEL ** -0.5),
        'w_e_up': nrm(30, (DEPTH, N_EXPERTS, D_MODEL, D_EXPERT), D_MODEL ** -0.5),
        'w_e_down': nrm(31, (DEPTH, N_EXPERTS, D_EXPERT, D_MODEL), D_EXPERT ** -0.5),
    }


def reference(x_prompt, x_sample, cache_k, cache_v, state_rnn, c, c_ctx, w_mod, b_mod,
              g_pre1, g_post1, g_pre2, g_post2, w_in, conv_w, conv_b, lru_wr, lru_br,
              lru_wi, lru_bi, lru_lam, lam_q1, lam_k1, lam_q2, lam_k2, g_sub, w_a_out,
              w_b_out, w_o, w_router, w_e_gate, w_e_up, w_e_down):
    n_lat = x_sample.shape[1]
    rope = axial_rope(n_lat)
    B_ctx = x_prompt.shape[0]
    B_dec = x_sample.shape[0]
    L_past = cache_k.shape[2]

    xp = x_prompt
    xs = x_sample
    ks_out, vs_out, st_out = [], [], []
    for l in range(DEPTH):
        lp = {
            'g_pre1': g_pre1[l], 'g_post1': g_post1[l], 'g_pre2': g_pre2[l], 'g_post2': g_post2[l],
            'w_in': w_in[l], 'conv_w': conv_w[l], 'conv_b': conv_b[l],
            'lru_wr': lru_wr[l], 'lru_br': lru_br[l], 'lru_wi': lru_wi[l], 'lru_bi': lru_bi[l],
            'lru_lam': lru_lam[l], 'lam_q1': lam_q1[l], 'lam_k1': lam_k1[l],
            'lam_q2': lam_q2[l], 'lam_k2': lam_k2[l], 'g_sub': g_sub[l],
            'w_a_out': w_a_out[l], 'w_b_out': w_b_out[l], 'w_o': w_o[l],
            'w_router': w_router[l], 'w_e_gate': w_e_gate[l], 'w_e_up': w_e_up[l], 'w_e_down': w_e_down[l],
        }
        mod_ctx = modulation(c_ctx[None, :], w_mod[l], b_mod[l])
        h0_ctx = jnp.zeros((B_ctx, 2, D_RNN), jnp.float32)
        xp, k_c, v_c, h_c = trunk_layer(xp, mod_ctx, lp, l, h0_ctx, None, None, None)
        ks_out.append(k_c)
        vs_out.append(v_c)
        st_out.append(h_c.astype(state_rnn.dtype))
        mod_lat = modulation(c, w_mod[l], b_mod[l])
        ctx_k = cache_k[:, l].reshape(B_dec, L_past, N_HEADS, 2, HD)
        ctx_v = cache_v[:, l]
        xs, _, _, _ = trunk_layer(xs, mod_lat, lp, l, state_rnn[:, l], ctx_k, ctx_v, rope)

    new_cache_k = jnp.stack(ks_out, axis=1)
    new_cache_v = jnp.stack(vs_out, axis=1)
    new_state_rnn = jnp.stack(st_out, axis=1)
    return (xp, xs, new_cache_k, new_cache_v, new_state_rnn)
```

```python
import functools
import math

import jax
import jax.numpy as jnp
from jax import lax
from jax.experimental import pallas as pl
from jax.experimental.pallas import tpu as pltpu

F32 = jnp.float32
BF16 = jnp.bfloat16
HIGHEST = lax.Precision.HIGHEST

D_MODEL = 2048
D_RNN = 1024
RNN_BLOCKS = 8
RNN_BS = 128
CONV_W = 4
LRU_C = 8.0
N_HEADS = 8
HD = 64
VD = 128
N_EXPERTS = 16
D_EXPERT = 1024
EC_FACTOR = 2
EPS = 1e-6
IN_COLS = 9216
GRID_W = 64
ROPE_BASE = 10000.0
LAM_INIT = 0.8 - 0.6 * math.exp(-0.3 * 0)

GROUP_TOKENS = 4096
GROUP_ROWS = 512
SUBLANES = 8
LANES = 128
V7X_VMEM_BYTES = 64 * 1024 * 1024


def _cparams(semantics, vmem_mb):
    return pltpu.CompilerParams(dimension_semantics=semantics, vmem_limit_bytes=vmem_mb * 1024 * 1024)


def _rms(x, g):
    return x * lax.rsqrt(jnp.mean(x * x, axis=-1, keepdims=True) + EPS) * g


def _mod_kernel(c_ref, w_ref, b_ref, o_ref):
    c = c_ref[...]
    s = c * jax.nn.sigmoid(c)
    o_ref[...] = jnp.dot(s, w_ref[...], precision=HIGHEST, preferred_element_type=F32) + b_ref[...]


def _modulation(cvec, w_mod, b_mod):
    tn = 1024
    return pl.pallas_call(
        _mod_kernel,
        grid=(6 * D_MODEL // tn,),
        in_specs=[
            pl.BlockSpec((SUBLANES, D_MODEL), lambda j: (0, 0)),
            pl.BlockSpec((D_MODEL, tn), lambda j: (0, j)),
            pl.BlockSpec((1, tn), lambda j: (0, j)),
        ],
        out_specs=pl.BlockSpec((SUBLANES, tn), lambda j: (0, j)),
        out_shape=jax.ShapeDtypeStruct((SUBLANES, 6 * D_MODEL), F32),
        compiler_params=_cparams(("arbitrary",), 40),
        name="modulation",
    )(cvec, w_mod, b_mod)


def _in_kernel(x_ref, sh_ref, sc_ref, g_ref, w_ref, o_ref, h_ref):
    @pl.when(pl.program_id(1) == 0)
    def _():
        h = _rms(x_ref[...], g_ref[...]) * (1.0 + sc_ref[...]) + sh_ref[...]
        h_ref[...] = h.astype(BF16)

    o_ref[...] = jnp.dot(h_ref[...], w_ref[...], preferred_element_type=F32)


def _in_proj(x_all, mod3, g_pre1, w_in_bf):
    T = x_all.shape[0]
    tm, tn = 1024, 1024
    tpg = GROUP_TOKENS // tm
    return pl.pallas_call(
        _in_kernel,
        grid=(T // tm, IN_COLS // tn),
        in_specs=[
            pl.BlockSpec((tm, D_MODEL), lambda i, j: (i, 0)),
            pl.BlockSpec((None, 1, D_MODEL), lambda i, j: (i // tpg, 0, 0)),
            pl.BlockSpec((None, 1, D_MODEL), lambda i, j: (i // tpg, 0, 1)),
            pl.BlockSpec((1, D_MODEL), lambda i, j: (0, 0)),
            pl.BlockSpec((D_MODEL, tn), lambda i, j: (0, j)),
        ],
        out_specs=pl.BlockSpec((tm, tn), lambda i, j: (i, j)),
        out_shape=jax.ShapeDtypeStruct((T, IN_COLS), F32),
        scratch_shapes=[pltpu.VMEM((tm, D_MODEL), BF16)],
        compiler_params=_cparams(("arbitrary", "arbitrary"), 52),
        name="in_proj",
    )(x_all, mod3, mod3, g_pre1, w_in_bf)


def _gelu_tanh(x):
    return 0.5 * x * (1.0 + jnp.tanh(math.sqrt(2.0 / math.pi) * (x + 0.044715 * (x * x * x))))


def _softplus(x):
    return jnp.maximum(x, 0.0) + jnp.log(1.0 + jnp.exp(-jnp.abs(x)))


def _split_bf16(x):
    hi = x.astype(BF16)
    return hi, (x - hi.astype(F32)).astype(BF16)


def _dot_3pass(x, w_hi, w_lo):
    x_hi, x_lo = _split_bf16(x)
    d = functools.partial(jnp.dot, preferred_element_type=F32)
    return d(x_hi, w_hi) + (d(x_lo, w_hi) + d(x_hi, w_lo))


def _rnn_kernel(xr_ref, gr_ref, cw_ref, cb_ref, wr_ref, wi_ref, br_ref, bi_ref, lam_ref, h0_ref,
                ya_ref, hf_ref, xp_ref, a_ref, b_ref, *, seq, nseq):
    n = seq * nseq
    seg = n // SUBLANES
    pitch = seg + SUBLANES
    zeros8 = jnp.zeros((SUBLANES, LANES), F32)
    xp_ref[0:SUBLANES, :] = zeros8
    xp_ref[SUBLANES + n:2 * SUBLANES + n, :] = zeros8
    xp_ref[SUBLANES:SUBLANES + n, :] = xr_ref[...]
    cw = cw_ref[...]
    cb = cb_ref[...]
    sp = _softplus(-lam_ref[...])
    w_r = [_split_bf16(wr_ref[d]) for d in range(2)]
    w_i = [_split_bf16(wi_ref[d]) for d in range(2)]

    def gates(s, carry):
        r0 = pl.multiple_of(s * seg, SUBLANES)
        t = (r0 + lax.broadcasted_iota(jnp.int32, (seg, LANES), 0)) % seq
        xc = cb + jnp.where(t >= 2, xp_ref[pl.ds(r0 + 6, seg), :], 0.0) * cw[0:1]
        xc = xc + jnp.where(t >= 1, xp_ref[pl.ds(r0 + 7, seg), :], 0.0) * cw[1:2]
        xc = xc + xp_ref[pl.ds(r0 + 8, seg), :] * cw[2:3]
        xc = xc + jnp.where(t <= seq - 2, xp_ref[pl.ds(r0 + 9, seg), :], 0.0) * cw[3:4]
        p0 = pl.multiple_of(s * pitch, SUBLANES)
        for d in range(2):
            r = jax.nn.sigmoid(_dot_3pass(xc, *w_r[d]) + br_ref[d:d + 1, :])
            i = jax.nn.sigmoid(_dot_3pass(xc, *w_i[d]) + bi_ref[d:d + 1, :])
            a = jnp.exp((-LRU_C) * r * sp[d:d + 1, :])
            a_ref[d, pl.ds(p0, seg), :] = a
            b_ref[d, pl.ds(p0, seg), :] = jnp.sqrt(1.0 - a * a) * (i * xc)
        return carry

    lax.fori_loop(0, SUBLANES, gates, 0)

    def scan(j, carry):
        hf, af, hb, ab = carry
        a0 = a_ref[0, pl.ds(j, SUBLANES, stride=pitch), :]
        b0 = b_ref[0, pl.ds(j, SUBLANES, stride=pitch), :]
        hf = a0 * hf + b0
        af = a0 * af
        b_ref[0, pl.ds(j, SUBLANES, stride=pitch), :] = hf
        a_ref[0, pl.ds(j, SUBLANES, stride=pitch), :] = af
        jb = seg - 1 - j
        a1 = a_ref[1, pl.ds(jb, SUBLANES, stride=pitch), :]
        b1 = b_ref[1, pl.ds(jb, SUBLANES, stride=pitch), :]
        hb = a1 * hb + b1
        ab = a1 * ab
        b_ref[1, pl.ds(jb, SUBLANES, stride=pitch), :] = hb
        a_ref[1, pl.ds(jb, SUBLANES, stride=pitch), :] = ab
        return hf, af, hb, ab

    ones8 = jnp.ones((SUBLANES, LANES), F32)
    ef, pf, eb, pb = lax.fori_loop(0, seg, scan, (zeros8, ones8, zeros8, ones8))

    spq = SUBLANES // nseq
    cf = [None] * SUBLANES
    cbk = [None] * SUBLANES
    for q in range(nseq):
        h0 = h0_ref[q]
        first, last = q * spq, q * spq + spq - 1
        cf[first] = h0[0:1, :]
        for s in range(first + 1, last + 1):
            cf[s] = ef[s - 1:s, :] + pf[s - 1:s, :] * cf[s - 1]
        cbk[last] = h0[1:2, :]
        for s in range(last - 1, first - 1, -1):
            cbk[s] = eb[s + 1:s + 2, :] + pb[s + 1:s + 2, :] * cbk[s + 1]
        hf_ref[q, 0:1, :] = ef[last:last + 1, :] + pf[last:last + 1, :] * cf[last]
        hf_ref[q, 1:2, :] = eb[first:first + 1, :] + pb[first:first + 1, :] * cbk[first]

    for s in range(SUBLANES):
        hfw = b_ref[0, s * pitch:s * pitch + seg, :] + a_ref[0, s * pitch:s * pitch + seg, :] * cf[s]
        hbw = b_ref[1, s * pitch:s * pitch + seg, :] + a_ref[1, s * pitch:s * pitch + seg, :] * cbk[s]
        y = (hfw + hbw) * _gelu_tanh(gr_ref[s * seg:(s + 1) * seg, :])
        ya_ref[s * seg:(s + 1) * seg, :] = y.astype(ya_ref.dtype)


def _rnn_branch(proj, conv_w, conv_b, wr, wi, br, bi, lam, h0, *, seq, nseq, nb, row0):
    n = seq * nseq
    seg = n // SUBLANES
    pitch = seg + SUBLANES
    return pl.pallas_call(
        functools.partial(_rnn_kernel, seq=seq, nseq=nseq),
        grid=(nb, RNN_BLOCKS),
        in_specs=[
            pl.BlockSpec((n, RNN_BS), lambda b, k: (row0 + b, k)),
            pl.BlockSpec((n, RNN_BS), lambda b, k: (row0 + b, RNN_BLOCKS + k)),
            pl.BlockSpec((CONV_W, RNN_BS), lambda b, k: (0, k)),
            pl.BlockSpec((1, RNN_BS), lambda b, k: (0, k)),
            pl.BlockSpec((2, None, RNN_BS, RNN_BS), lambda b, k: (0, k, 0, 0)),
            pl.BlockSpec((2, None, RNN_BS, RNN_BS), lambda b, k: (0, k, 0, 0)),
            pl.BlockSpec((2, RNN_BS), lambda b, k: (0, k)),
            pl.BlockSpec((2, RNN_BS), lambda b, k: (0, k)),
            pl.BlockSpec((2, RNN_BS), lambda b, k: (0, k)),
            pl.BlockSpec((nseq, 2, RNN_BS), lambda b, k: (b, 0, k)),
        ],
        out_specs=[
            pl.BlockSpec((n, RNN_BS), lambda b, k: (b, k)),
            pl.BlockSpec((nseq, 2, RNN_BS), lambda b, k: (b, 0, k)),
        ],
        out_shape=[jax.ShapeDtypeStruct((nb * n, D_RNN), BF16), jax.ShapeDtypeStruct((nb * nseq, 2, D_RNN), F32)],
        scratch_shapes=[
            pltpu.VMEM((n + 2 * SUBLANES, RNN_BS), F32),
            pltpu.VMEM((2, SUBLANES * pitch, RNN_BS), F32),
            pltpu.VMEM((2, SUBLANES * pitch, RNN_BS), F32),
        ],
        compiler_params=_cparams(("arbitrary", "arbitrary"), 40),
        name=f"rnn_n{n}",
    )(proj, proj, conv_w, conv_b, wr, wi, br, bi, lam, h0)


def _rope(x, cos, sin_signed):
    lane = lax.broadcasted_iota(jnp.int32, x.shape, 1)
    partner = jnp.where((lane % HD) < HD // 2, pltpu.roll(x, LANES - HD // 2, 1), pltpu.roll(x, HD // 2, 1))
    return x * cos + partner * sin_signed


def _attn_kernel(*refs, n, npast, tq, rope):
    if rope:
        (lq1, lk1, lq2, lk2, q_ref, k_ref, v_ref, gsub_ref, ck_ref, cv_ref, cosk_ref, sink_ref, cosq_ref, sinq_ref,
         o_ref, kb_ref, vb_ref) = refs
    else:
        lq1, lk1, lq2, lk2, q_ref, k_ref, v_ref, gsub_ref, o_ref, kb_ref, vb_ref = refs
    chunk = 256

    @pl.when(pl.program_id(2) == 0)
    def _():
        if npast:
            kb_ref[0:npast, :] = ck_ref[...].astype(BF16)
            vb_ref[0:npast, :] = cv_ref[...].astype(BF16)

        def fill(c, carry):
            r0 = pl.multiple_of(c * chunk, chunk)
            k = k_ref[pl.ds(r0, chunk), :]
            if rope:
                k = _rope(k, cosk_ref[pl.ds(r0, chunk), :], sink_ref[pl.ds(r0, chunk), :])
            kb_ref[pl.ds(npast + r0, chunk), :] = k.astype(BF16)
            vb_ref[pl.ds(npast + r0, chunk), :] = v_ref[pl.ds(r0, chunk), :].astype(BF16)
            return carry

        lax.fori_loop(0, n // chunk, fill, 0)

    lam = (jnp.exp(jnp.sum(lq1[...] * lk1[...], axis=-1, keepdims=True))
           - jnp.exp(jnp.sum(lq2[...] * lk2[...], axis=-1, keepdims=True)) + LAM_INIT)
    q = q_ref[...]
    if rope:
        q = _rope(q, cosq_ref[...], sinq_ref[...])
    q = q * (HD ** -0.5 * math.log2(math.e))
    lane = lax.broadcasted_iota(jnp.int32, q.shape, 1)
    zero = jnp.zeros_like(q)
    q2 = jnp.concatenate([jnp.where(lane < HD, q, zero), jnp.where(lane < HD, zero, q)], axis=0).astype(BF16)
    s = lax.dot_general(q2, kb_ref[...], (((1,), (1,)), ((), ())), preferred_element_type=F32)
    p = jnp.exp2(s - jnp.max(s, axis=-1, keepdims=True))
    rl = 1.0 / jnp.sum(p, axis=-1, keepdims=True)
    pd = (p[0:tq] * rl[0:tq] - p[tq:2 * tq] * (lam * rl[tq:2 * tq])).astype(BF16)
    o = jnp.dot(pd, vb_ref[...], preferred_element_type=F32)
    o = _rms(o, gsub_ref[...]) * (1.0 - LAM_INIT)
    o_ref[...] = o.astype(o_ref.dtype)


def _attention(proj, lam_vecs, g_sub, *, n, nb, row0, tq, cache=None, tables=None):
    rope = tables is not None
    npast = 0 if cache is None else cache[0].shape[1]
    nk = n + npast
    qpr = n // tq
    qcol, kcol, vcol = 2 * RNN_BLOCKS, 2 * RNN_BLOCKS + N_HEADS, 2 * RNN_BLOCKS + 2 * N_HEADS
    vec = pl.BlockSpec((1, HD), lambda b, h, t: (0, 0))
    in_specs = [vec, vec, vec, vec,
                pl.BlockSpec((tq, VD), lambda b, h, t: ((row0 + b) * qpr + t, qcol + h)),
                pl.BlockSpec((n, VD), lambda b, h, t: (row0 + b, kcol + h)),
                pl.BlockSpec((n, VD), lambda b, h, t: (row0 + b, vcol + h)),
                pl.BlockSpec((1, VD), lambda b, h, t: (0, 0))]
    args = [*lam_vecs, proj, proj, proj, g_sub]
    if rope:
        ck, cv = cache
        cos_t, sin_t = tables
        in_specs += [pl.BlockSpec((None, npast, VD), lambda b, h, t: (b, 0, h)),
                     pl.BlockSpec((None, npast, VD), lambda b, h, t: (b, 0, h)),
                     pl.BlockSpec((n, VD), lambda b, h, t: (0, 0), pipeline_mode=pl.Buffered(1)),
                     pl.BlockSpec((n, VD), lambda b, h, t: (0, 0), pipeline_mode=pl.Buffered(1)),
                     pl.BlockSpec((tq, VD), lambda b, h, t: (t, 0)),
                     pl.BlockSpec((tq, VD), lambda b, h, t: (t, 0))]
        args += [ck, cv, cos_t, sin_t, cos_t, sin_t]
    return pl.pallas_call(
        functools.partial(_attn_kernel, n=n, npast=npast, tq=tq, rope=rope),
        grid=(nb, N_HEADS, qpr),
        in_specs=in_specs,
        out_specs=pl.BlockSpec((tq, VD), lambda b, h, t: (b * qpr + t, h)),
        out_shape=jax.ShapeDtypeStruct((nb * n, N_HEADS * VD), BF16),
        scratch_shapes=[pltpu.VMEM((nk, VD), BF16), pltpu.VMEM((nk, VD), BF16)],
        compiler_params=_cparams(("arbitrary", "arbitrary", "arbitrary"), 56),
        name=f"attn_n{n}",
    )(*args)


def _post_kernel(ya_ref, on_ref, ga0_ref, ga1_ref, gb0_ref, gb1_ref, x_ref, g1_ref, sh2_ref, sc2_ref,
                 gpost1_ref, gpre2_ref, wa_ref, wb_ref, wo_ref, wrt_ref, x1_ref, h2_ref, aff_ref, m_ref):
    oa = jnp.dot(ya_ref[...], wa_ref[...], preferred_element_type=F32)
    ob = jnp.dot(on_ref[...], wb_ref[...], preferred_element_type=F32)
    half = D_MODEL // 2
    m_ref[:, 0:half] = (jax.nn.sigmoid(ga0_ref[...]) * oa[:, 0:half]
                        + jax.nn.sigmoid(gb0_ref[...]) * ob[:, 0:half]).astype(BF16)
    m_ref[:, half:] = (jax.nn.sigmoid(ga1_ref[...]) * oa[:, half:]
                       + jax.nn.sigmoid(gb1_ref[...]) * ob[:, half:]).astype(BF16)
    t = jnp.dot(m_ref[...], wo_ref[...], preferred_element_type=F32)
    x1 = x_ref[...] + g1_ref[...] * _rms(t, gpost1_ref[...])
    x1_ref[...] = x1
    h2 = _rms(x1, gpre2_ref[...]) * (1.0 + sc2_ref[...]) + sh2_ref[...]
    h2_ref[...] = h2
    nt = functools.partial(lax.dot_general, dimension_numbers=(((1,), (1,)), ((), ())), preferred_element_type=F32)
    w_hi, w_lo = _split_bf16(wrt_ref[...])
    h_hi, h_lo = _split_bf16(h2)
    logits = nt(w_hi, h_hi) + (nt(w_lo, h_hi) + nt(w_hi, h_lo))
    e = jnp.exp(logits - jnp.max(logits, axis=0, keepdims=True))
    aff_ref[...] = e / jnp.sum(e, axis=0, keepdims=True)


def _post_mix(ya, on, proj, x_all, mod3, g_post1, g_pre2, wa, wb, wo, wrt):
    T = x_all.shape[0]
    tm = 256
    tpg = GROUP_TOKENS // tm
    half = D_MODEL // 2
    gate0 = (2 * D_RNN + 2 * N_HEADS * VD + N_HEADS * VD) // half
    row = lambda c: pl.BlockSpec((None, 1, D_MODEL), lambda i: (i // tpg, 0, c))
    const = lambda shape: pl.BlockSpec(shape, lambda i: (0,) * len(shape), pipeline_mode=pl.Buffered(1))
    return pl.pallas_call(
        _post_kernel,
        grid=(T // tm,),
        in_specs=[
            pl.BlockSpec((tm, D_RNN), lambda i: (i, 0)),
            pl.BlockSpec((tm, N_HEADS * VD), lambda i: (i, 0)),
            pl.BlockSpec((tm, half), lambda i: (i, gate0)),
            pl.BlockSpec((tm, half), lambda i: (i, gate0 + 1)),
            pl.BlockSpec((tm, half), lambda i: (i, gate0 + 2)),
            pl.BlockSpec((tm, half), lambda i: (i, gate0 + 3)),
            pl.BlockSpec((tm, D_MODEL), lambda i: (i, 0)),
            row(2), row(3), row(4),
            const((1, D_MODEL)), const((1, D_MODEL)),
            const((D_RNN, D_MODEL)), const((N_HEADS * VD, D_MODEL)), const((D_MODEL, D_MODEL)),
            const((N_EXPERTS, D_MODEL)),
        ],
        out_specs=[
            pl.BlockSpec((tm, D_MODEL), lambda i: (i, 0)),
            pl.BlockSpec((tm, D_MODEL), lambda i: (i, 0)),
            pl.BlockSpec((N_EXPERTS, tm), lambda i: (0, i)),
        ],
        out_shape=[jax.ShapeDtypeStruct((T, D_MODEL), F32), jax.ShapeDtypeStruct((T, D_MODEL), F32),
                   jax.ShapeDtypeStruct((N_EXPERTS, T), F32)],
        scratch_shapes=[pltpu.VMEM((tm, D_MODEL), BF16)],
        compiler_params=_cparams(("arbitrary",), 56),
        name="post_mix",
    )(ya, on, proj, proj, proj, proj, x_all, mod3, mod3, mod3, g_post1, g_pre2, wa, wb, wo, wrt)


def _cumsum_excl(x, tri):
    n = x.shape[1]
    blk = tri.shape[0]
    outs = []
    carry = jnp.zeros((x.shape[0], 1), F32)
    for c in range(n // blk):
        xc = x[:, c * blk:(c + 1) * blk]
        outs.append(jnp.dot(xc.astype(BF16), tri, preferred_element_type=F32) + carry)
        carry = carry + jnp.sum(xc, axis=1, keepdims=True)
    return outs[0] if len(outs) == 1 else jnp.concatenate(outs, axis=1)


def _route_kernel(aff_ref, idx_ref, val_ref, growt_ref, *, n, cap, row_stride):
    aff = aff_ref[...]

    def bisect(_, lohi):
        lo, hi = lohi
        mid = lo + ((hi - lo) >> 1)
        cnt = jnp.sum((aff >= pltpu.bitcast(mid, F32)).astype(jnp.int32), axis=1, keepdims=True)
        ok = cnt >= cap
        return jnp.where(ok, mid, lo), jnp.where(ok, hi, mid)

    lo0 = jnp.zeros((N_EXPERTS, 1), jnp.int32)
    hi0 = jnp.full((N_EXPERTS, 1), 0x7F800000, jnp.int32)
    thr, _ = lax.fori_loop(0, 31, bisect, (lo0, hi0))
    gt = aff >= pltpu.bitcast(thr + 1, F32)
    eq = (aff >= pltpu.bitcast(thr, F32)) & jnp.logical_not(gt)
    need = cap - jnp.sum(gt.astype(jnp.int32), axis=1, keepdims=True)
    blk = min(n, 256)
    tri = (lax.broadcasted_iota(jnp.int32, (blk, blk), 0) < lax.broadcasted_iota(jnp.int32, (blk, blk), 1)).astype(BF16)
    eqpos = _cumsum_excl(eq.astype(F32), tri)
    sel = gt | (eq & (eqpos < need.astype(F32)))
    pos = _cumsum_excl(sel.astype(F32), tri).astype(jnp.int32)
    slot = jnp.where(sel, pos, -1)
    grow = jnp.where(sel, pos + pl.program_id(0) * row_stride, -1)
    growt_ref[...] = grow.astype(F32).T.astype(jnp.int32)

    rc = min(cap, 32)
    tok = lax.broadcasted_iota(jnp.int32, (rc, n), 1)
    for e in range(N_EXPERTS):
        slot_e = slot[e:e + 1, :]
        aff_e = aff[e:e + 1, :]

        def extract(c, carry):
            r0 = pl.multiple_of(c * rc, rc)
            hit = slot_e == (lax.broadcasted_iota(jnp.int32, (rc, n), 0) + r0)
            idx_ref[e, pl.ds(r0, rc), :] = jnp.sum(jnp.where(hit, tok, 0), axis=1, keepdims=True)
            val_ref[e, pl.ds(r0, rc), :] = jnp.sum(jnp.where(hit, aff_e, 0.0), axis=1, keepdims=True)
            return carry

        lax.fori_loop(0, cap // rc, extract, 0)


def _route(aff_t, *, n, nsets, col0, row_stride):
    cap = EC_FACTOR * n // N_EXPERTS
    kern = functools.partial(_route_kernel, n=n, cap=cap, row_stride=row_stride)
    return pl.pallas_call(
        kern,
        grid=(nsets,),
        in_specs=[pl.BlockSpec((N_EXPERTS, n), lambda s: (0, col0 + s))],
        out_specs=[
            pl.BlockSpec((None, N_EXPERTS, cap, 1), lambda s: (s, 0, 0, 0)),
            pl.BlockSpec((None, N_EXPERTS, cap, 1), lambda s: (s, 0, 0, 0)),
            pl.BlockSpec((n, N_EXPERTS), lambda s: (s, 0)),
        ],
        out_shape=[jax.ShapeDtypeStruct((nsets, N_EXPERTS, cap, 1), jnp.int32),
                   jax.ShapeDtypeStruct((nsets, N_EXPERTS, cap, 1), F32),
                   jax.ShapeDtypeStruct((nsets * n, N_EXPERTS), jnp.int32)],
        compiler_params=_cparams(("arbitrary",), 48),
        name=f"route_n{n}",
    )(aff_t)


def _expert_kernel(cur_ref, nxt_ref, h2_hbm, val_ref, wg_ref, wu_ref, wd_ref, ye_ref, xs_ref, sem, *, rows):
    step = pl.program_id(0) * pl.num_programs(1) + pl.program_id(1)
    nsteps = pl.num_programs(0) * pl.num_programs(1)
    slot = step % 2

    def gather(idx_ref, s):
        def issue(r, carry):
            pltpu.make_async_copy(h2_hbm.at[pl.ds(idx_ref[0, r], 1), :], xs_ref.at[s, pl.ds(r, 1), :],
                                  sem.at[s]).start()
            return carry

        lax.fori_loop(0, rows, issue, 0, unroll=8)

    @pl.when(step == 0)
    def _():
        gather(cur_ref, 0)

    @pl.when(step + 1 < nsteps)
    def _():
        gather(nxt_ref, 1 - slot)

    pltpu.make_async_copy(h2_hbm.at[pl.ds(0, rows), :], xs_ref.at[slot], sem.at[slot]).wait()
    half = rows // 2
    for c in range(2):
        xs = xs_ref[slot, c * half:(c + 1) * half, :].astype(BF16)
        g = jnp.dot(xs, wg_ref[...], preferred_element_type=F32)
        u = jnp.dot(xs, wu_ref[...], preferred_element_type=F32)
        hid = (g * jax.nn.sigmoid(g)) * u
        y = jnp.dot(hid.astype(BF16), wd_ref[...], preferred_element_type=F32) * val_ref[c * half:(c + 1) * half, :]
        ye_ref[c * half:(c + 1) * half, :] = y.astype(ye_ref.dtype)


def _experts(gidx, h2, vals, wg, wu, wd):
    rows_per_expert = vals.shape[1]
    rows = GROUP_ROWS
    groups = rows_per_expert // rows
    last = N_EXPERTS * groups - 1
    return pl.pallas_call(
        functools.partial(_expert_kernel, rows=rows),
        grid=(N_EXPERTS, groups),
        in_specs=[
            pl.BlockSpec((None, 1, rows), lambda e, g: (e * groups + g, 0, 0), memory_space=pltpu.SMEM),
            pl.BlockSpec((None, 1, rows), lambda e, g: (jnp.minimum(e * groups + g + 1, last), 0, 0),
                         memory_space=pltpu.SMEM),
            pl.BlockSpec(memory_space=pl.ANY),
            pl.BlockSpec((None, rows, 1), lambda e, g: (e, g, 0)),
            pl.BlockSpec((None, D_MODEL, D_EXPERT), lambda e, g: (e, 0, 0)),
            pl.BlockSpec((None, D_MODEL, D_EXPERT), lambda e, g: (e, 0, 0)),
            pl.BlockSpec((None, D_EXPERT, D_MODEL), lambda e, g: (e, 0, 0)),
        ],
        out_specs=pl.BlockSpec((None, rows, D_MODEL), lambda e, g: (e, g, 0)),
        out_shape=jax.ShapeDtypeStruct((N_EXPERTS, rows_per_expert, D_MODEL), BF16),
        scratch_shapes=[pltpu.VMEM((2, rows, D_MODEL), F32), pltpu.SemaphoreType.DMA((2,))],
        compiler_params=_cparams(("arbitrary", "arbitrary"), 56),
        name="experts",
    )(gidx.reshape(N_EXPERTS * groups, 1, rows), gidx.reshape(N_EXPERTS * groups, 1, rows), h2, vals, wg, wu, wd)


def _combine_kernel(bounds_ref, growt_ref, ye_ref, x1_ref, g2_ref, gpost2_ref, o_ref, acc_ref):
    e = pl.program_id(1)

    @pl.when(e == 0)
    def _():
        acc_ref[...] = jnp.zeros_like(acc_ref)

    growt = growt_ref[...]
    lane = lax.broadcasted_iota(jnp.int32, growt.shape, 1)
    col = jnp.sum(jnp.where(lane == e, growt, 0), axis=1, keepdims=True)
    base = (pl.program_id(0) * N_EXPERTS + e) * 2
    lo, hi = bounds_ref[base], bounds_ref[base + 1]
    half = GROUP_ROWS // 2
    for c in range(2):
        @pl.when((lo < (c + 1) * half) & (hi > c * half))
        def _():
            hit = col == (lax.broadcasted_iota(jnp.int32, (growt.shape[0], half), 1) + c * half)
            onehot = jnp.where(hit, 1.0, 0.0).astype(BF16)
            acc_ref[...] += jnp.dot(onehot, ye_ref[c * half:(c + 1) * half, :], preferred_element_type=F32)

    @pl.when(e == N_EXPERTS - 1)
    def _():
        o_ref[...] = x1_ref[...] + g2_ref[...] * _rms(acc_ref[...], gpost2_ref[...])


def _combine(bounds, growt, ye, x1, mod3, g_post2):
    T = x1.shape[0]
    tm = 512
    tpg = GROUP_TOKENS // tm
    return pl.pallas_call(
        _combine_kernel,
        grid_spec=pltpu.PrefetchScalarGridSpec(
            num_scalar_prefetch=1,
            grid=(T // tm, N_EXPERTS),
            in_specs=[
                pl.BlockSpec((tm, N_EXPERTS), lambda i, e, b: (i, 0)),
                pl.BlockSpec((None, GROUP_ROWS, D_MODEL), lambda i, e, b: (e, i // tpg, 0)),
                pl.BlockSpec((tm, D_MODEL), lambda i, e, b: (i, 0)),
                pl.BlockSpec((None, 1, D_MODEL), lambda i, e, b: (i // tpg, 0, 5)),
                pl.BlockSpec((1, D_MODEL), lambda i, e, b: (0, 0)),
            ],
            out_specs=pl.BlockSpec((tm, D_MODEL), lambda i, e, b: (i, 0)),
            scratch_shapes=[pltpu.VMEM((tm, D_MODEL), F32)],
        ),
        out_shape=jax.ShapeDtypeStruct((T, D_MODEL), F32),
        compiler_params=_cparams(("arbitrary", "arbitrary"), 48),
        name="combine",
    )(bounds, growt, ye, x1, mod3, g_post2)


def _tile_row_bounds(growt, tm):
    g = growt.reshape(-1, tm, N_EXPERTS)
    lo = jnp.min(jnp.where(g >= 0, g, GROUP_ROWS), axis=1)
    hi = jnp.max(g, axis=1) + 1
    return jnp.stack([lo, hi], axis=-1).reshape(-1).astype(jnp.int32)


def _rope_tables(n):
    rows = n // GRID_W
    row = jnp.repeat(jnp.arange(rows), GRID_W).astype(F32)
    col = jnp.tile(jnp.arange(GRID_W), rows).astype(F32)
    inv = ROPE_BASE ** (-jnp.arange(0, HD // 2, 2, dtype=F32) / (HD // 2))
    ang = jnp.concatenate([row[:, None] * inv, col[:, None] * inv], axis=-1)
    cos, sin = jnp.cos(ang), jnp.sin(ang)
    return jnp.tile(cos, (1, 4)), jnp.tile(jnp.concatenate([-sin, sin], axis=-1), (1, 2))


def kernel(x_prompt, x_sample, cache_k, cache_v, state_rnn, c, c_ctx, w_mod, b_mod, g_pre1, g_post1, g_pre2, g_post2, w_in, conv_w, conv_b, lru_wr, lru_br, lru_wi, lru_bi, lru_lam, lam_q1, lam_k1, lam_q2, lam_k2, g_sub, w_a_out, w_b_out, w_o, w_router, w_e_gate, w_e_up, w_e_down):
    nbc, nc, _ = x_prompt.shape
    nbl, nl, _ = x_sample.shape
    assert nbc * nc == GROUP_TOKENS and nl == GROUP_TOKENS and w_mod.shape[0] == 1
    tc = nbc * nc
    x_all = jnp.concatenate([x_prompt.reshape(tc, D_MODEL), x_sample.reshape(nbl * nl, D_MODEL)], axis=0)

    cvec = jnp.concatenate([c_ctx[None, :], c, jnp.zeros((SUBLANES - 1 - nbl, D_MODEL), F32)], axis=0)
    mod = _modulation(cvec, w_mod[0], b_mod)
    mod3 = mod.reshape(SUBLANES, 1, 6 * D_MODEL)

    proj = _in_proj(x_all, mod3, g_pre1, w_in[0].astype(BF16))

    rnn_w = (conv_w[0], conv_b, lru_wr[0], lru_wi[0], lru_br[0], lru_bi[0], lru_lam[0])
    ya_c, h_fin = _rnn_branch(proj, *rnn_w, jnp.zeros((nbc, 2, D_RNN), F32), seq=nc, nseq=SUBLANES,
                              nb=nbc // SUBLANES, row0=0)
    ya_l, _ = _rnn_branch(proj, *rnn_w, state_rnn[:, 0], seq=nl, nseq=1, nb=nbl, row0=tc // nl)
    ya = jnp.concatenate([ya_c, ya_l], axis=0)

    lam_vecs = (lam_q1, lam_k1, lam_q2, lam_k2)
    on_c = _attention(proj, lam_vecs, g_sub, n=nc, nb=nbc, row0=0, tq=nc)
    past = cache_k.shape[2]
    on_l = _attention(proj, lam_vecs, g_sub, n=nl, nb=nbl, row0=tc // nl, tq=256,
                      cache=(cache_k[:, 0].reshape(nbl, past, N_HEADS * VD),
                             cache_v[:, 0].reshape(nbl, past, N_HEADS * VD)),
                      tables=_rope_tables(nl))
    on = jnp.concatenate([on_c, on_l], axis=0)

    x1, h2, aff_t = _post_mix(ya, on, proj, x_all, mod3, g_post1, g_pre2, w_a_out[0].astype(BF16),
                              w_b_out[0].astype(BF16), w_o[0].astype(BF16), w_router[0].T)

    capc = EC_FACTOR * nc // N_EXPERTS
    idx_c, val_c, growt_c = _route(aff_t, n=nc, nsets=nbc, col0=0, row_stride=capc)
    idx_l, val_l, growt_l = _route(aff_t, n=nl, nsets=nbl, col0=tc // nl, row_stride=0)
    gid_c = idx_c[..., 0] + (jnp.arange(nbc, dtype=jnp.int32) * nc)[:, None, None]
    gid_l = idx_l[..., 0] + (tc + jnp.arange(nbl, dtype=jnp.int32) * nl)[:, None, None]
    gidx = jnp.concatenate([gid_c.transpose(1, 0, 2).reshape(N_EXPERTS, -1),
                            gid_l.transpose(1, 0, 2).reshape(N_EXPERTS, -1)], axis=1)
    vals = jnp.concatenate([val_c.transpose(1, 0, 2, 3).reshape(N_EXPERTS, -1, 1),
                            val_l.transpose(1, 0, 2, 3).reshape(N_EXPERTS, -1, 1)], axis=1)
    growt = jnp.concatenate([growt_c, growt_l], axis=0)

    ye = _experts(gidx, h2, vals, w_e_gate[0].astype(BF16), w_e_up[0].astype(BF16),
                  w_e_down[0].astype(BF16))
    x2 = _combine(_tile_row_bounds(growt, 512), growt, ye, x1, mod3, g_post2)

    y_prompt = x2[:tc].reshape(nbc, nc, D_MODEL)
    y_sample = x2[tc:].reshape(nbl, nl, D_MODEL)
    kcol = (2 * RNN_BLOCKS + N_HEADS) * VD
    vcol = kcol + N_HEADS * VD
    new_k = proj[:tc, kcol:kcol + N_HEADS * VD].reshape(nbc, 1, nc, N_HEADS, 2 * HD)
    new_v = proj[:tc, vcol:vcol + N_HEADS * VD].reshape(nbc, 1, nc, N_HEADS, VD)
    return (y_prompt, y_sample, new_k, new_v, h_fin[:, None])
```

```python
import functools
import math

import jax
import jax.numpy as jnp
from jax import lax
from jax.experimental import pallas as pl
from jax.experimental.pallas import tpu as pltpu

F32 = jnp.float32
BF16 = jnp.bfloat16
HIGHEST = lax.Precision.HIGHEST

D_MODEL = 2048
D_RNN = 1024
RNN_BLOCKS = 8
RNN_BS = 128
CONV_W = 4
LRU_C = 8.0
N_HEADS = 8
HD = 64
VD = 128
N_EXPERTS = 16
D_EXPERT = 1024
EC_FACTOR = 2
EPS = 1e-6
IN_COLS = 9216
GRID_W = 64
ROPE_BASE = 10000.0
LAM_INIT = 0.8 - 0.6 * math.exp(-0.3 * 0)

GROUP_TOKENS = 4096
GROUP_ROWS = 512
SUBLANES = 8
LANES = 128
V7X_VMEM_BYTES = 64 * 1024 * 1024


def _cparams(semantics, vmem_mb):
    return pltpu.CompilerParams(dimension_semantics=semantics, vmem_limit_bytes=vmem_mb * 1024 * 1024)


def _rms(x, g):
    return x * lax.rsqrt(jnp.mean(x * x, axis=-1, keepdims=True) + EPS) * g


def _mod_kernel(c_ref, w_ref, b_ref, o_ref):
    c = c_ref[...]
    s = c * jax.nn.sigmoid(c)
    o_ref[...] = jnp.dot(s, w_ref[...], precision=HIGHEST, preferred_element_type=F32) + b_ref[...]


def _modulation(cvec, w_mod, b_mod):
    tn = 1024
    return pl.pallas_call(
        _mod_kernel,
        grid=(6 * D_MODEL // tn,),
        in_specs=[
            pl.BlockSpec((SUBLANES, D_MODEL), lambda j: (0, 0)),
            pl.BlockSpec((D_MODEL, tn), lambda j: (0, j)),
            pl.BlockSpec((1, tn), lambda j: (0, j)),
        ],
        out_specs=pl.BlockSpec((SUBLANES, tn), lambda j: (0, j)),
        out_shape=jax.ShapeDtypeStruct((SUBLANES, 6 * D_MODEL), F32),
        compiler_params=_cparams(("arbitrary",), 40),
        name="modulation",
    )(cvec, w_mod, b_mod)


def _in_kernel(x_ref, sh_ref, sc_ref, g_ref, w_ref, o_ref, h_ref):
    @pl.when(pl.program_id(1) == 0)
    def _():
        h = _rms(x_ref[...], g_ref[...]) * (1.0 + sc_ref[...]) + sh_ref[...]
        h_ref[...] = h.astype(BF16)

    o_ref[...] = jnp.dot(h_ref[...], w_ref[...], preferred_element_type=F32)


def _in_proj(x_all, mod3, g_pre1, w_in_bf):
    T = x_all.shape[0]
    tm, tn = 1024, 1024
    tpg = GROUP_TOKENS // tm
    return pl.pallas_call(
        _in_kernel,
        grid=(T // tm, IN_COLS // tn),
        in_specs=[
            pl.BlockSpec((tm, D_MODEL), lambda i, j: (i, 0)),
            pl.BlockSpec((None, 1, D_MODEL), lambda i, j: (i // tpg, 0, 0)),
            pl.BlockSpec((None, 1, D_MODEL), lambda i, j: (i // tpg, 0, 1)),
            pl.BlockSpec((1, D_MODEL), lambda i, j: (0, 0)),
            pl.BlockSpec((D_MODEL, tn), lambda i, j: (0, j)),
        ],
        out_specs=pl.BlockSpec((tm, tn), lambda i, j: (i, j)),
        out_shape=jax.ShapeDtypeStruct((T, IN_COLS), F32),
        scratch_shapes=[pltpu.VMEM((tm, D_MODEL), BF16)],
        compiler_params=_cparams(("arbitrary", "arbitrary"), 52),
        name="in_proj",
    )(x_all, mod3, mod3, g_pre1, w_in_bf)


def _gelu_tanh(x):
    return 0.5 * x * (1.0 + jnp.tanh(math.sqrt(2.0 / math.pi) * (x + 0.044715 * (x * x * x))))


def _softplus(x):
    return jnp.maximum(x, 0.0) + jnp.log(1.0 + jnp.exp(-jnp.abs(x)))


def _split_bf16(x):
    hi = x.astype(BF16)
    return hi, (x - hi.astype(F32)).astype(BF16)


def _dot_3pass(x, w_hi, w_lo):
    x_hi, x_lo = _split_bf16(x)
    d = functools.partial(jnp.dot, preferred_element_type=F32)
    return d(x_hi, w_hi) + (d(x_lo, w_hi) + d(x_hi, w_lo))


def _rnn_kernel(xr_ref, gr_ref, cw_ref, cb_ref, wr_ref, wi_ref, br_ref, bi_ref, lam_ref, h0_ref,
                ya_ref, hf_ref, xp_ref, a_ref, b_ref, *, seq, nseq):
    n = seq * nseq
    seg = n // SUBLANES
    pitch = seg + SUBLANES
    zeros8 = jnp.zeros((SUBLANES, LANES), F32)
    xp_ref[0:SUBLANES, :] = zeros8
    xp_ref[SUBLANES + n:2 * SUBLANES + n, :] = zeros8
    xp_ref[SUBLANES:SUBLANES + n, :] = xr_ref[...]
    cw = cw_ref[...]
    cb = cb_ref[...]
    sp = _softplus(-lam_ref[...])
    w_r = [_split_bf16(wr_ref[d]) for d in range(2)]
    w_i = [_split_bf16(wi_ref[d]) for d in range(2)]

    def gates(s, carry):
        r0 = pl.multiple_of(s * seg, SUBLANES)
        t = (r0 + lax.broadcasted_iota(jnp.int32, (seg, LANES), 0)) % seq
        xc = cb + jnp.where(t >= 2, xp_ref[pl.ds(r0 + 6, seg), :], 0.0) * cw[0:1]
        xc = xc + jnp.where(t >= 1, xp_ref[pl.ds(r0 + 7, seg), :], 0.0) * cw[1:2]
        xc = xc + xp_ref[pl.ds(r0 + 8, seg), :] * cw[2:3]
        xc = xc + jnp.where(t <= seq - 2, xp_ref[pl.ds(r0 + 9, seg), :], 0.0) * cw[3:4]
        p0 = pl.multiple_of(s * pitch, SUBLANES)
        for d in range(2):
            r = jax.nn.sigmoid(_dot_3pass(xc, *w_r[d]) + br_ref[d:d + 1, :])
            i = jax.nn.sigmoid(_dot_3pass(xc, *w_i[d]) + bi_ref[d:d + 1, :])
            a = jnp.exp((-LRU_C) * r * sp[d:d + 1, :])
            a_ref[d, pl.ds(p0, seg), :] = a
            b_ref[d, pl.ds(p0, seg), :] = jnp.sqrt(1.0 - a * a) * (i * xc)
        return carry

    lax.fori_loop(0, SUBLANES, gates, 0)

    def scan(j, carry):
        hf, af, hb, ab = carry
        a0 = a_ref[0, pl.ds(j, SUBLANES, stride=pitch), :]
        b0 = b_ref[0, pl.ds(j, SUBLANES, stride=pitch), :]
        hf = a0 * hf + b0
        af = a0 * af
        b_ref[0, pl.ds(j, SUBLANES, stride=pitch), :] = hf
        a_ref[0, pl.ds(j, SUBLANES, stride=pitch), :] = af
        jb = seg - 1 - j
        a1 = a_ref[1, pl.ds(jb, SUBLANES, stride=pitch), :]
        b1 = b_ref[1, pl.ds(jb, SUBLANES, stride=pitch), :]
        hb = a1 * hb + b1
        ab = a1 * ab
        b_ref[1, pl.ds(jb, SUBLANES, stride=pitch), :] = hb
        a_ref[1, pl.ds(jb, SUBLANES, stride=pitch), :] = ab
        return hf, af, hb, ab

    ones8 = jnp.ones((SUBLANES, LANES), F32)
    ef, pf, eb, pb = lax.fori_loop(0, seg, scan, (zeros8, ones8, zeros8, ones8))

    spq = SUBLANES // nseq
    cf = [None] * SUBLANES
    cbk = [None] * SUBLANES
    for q in range(nseq):
        h0 = h0_ref[q]
        first, last = q * spq, q * spq + spq - 1
        cf[first] = h0[0:1, :]
        for s in range(first + 1, last + 1):
            cf[s] = ef[s - 1:s, :] + pf[s - 1:s, :] * cf[s - 1]
        cbk[last] = h0[1:2, :]
        for s in range(last - 1, first - 1, -1):
            cbk[s] = eb[s + 1:s + 2, :] + pb[s + 1:s + 2, :] * cbk[s + 1]
        hf_ref[q, 0:1, :] = ef[last:last + 1, :] + pf[last:last + 1, :] * cf[last]
        hf_ref[q, 1:2, :] = eb[first:first + 1, :] + pb[first:first + 1, :] * cbk[first]

    for s in range(SUBLANES):
        hfw = b_ref[0, s * pitch:s * pitch + seg, :] + a_ref[0, s * pitch:s * pitch + seg, :] * cf[s]
        hbw = b_ref[1, s * pitch:s * pitch + seg, :] + a_ref[1, s * pitch:s * pitch + seg, :] * cbk[s]
        y = (hfw + hbw) * _gelu_tanh(gr_ref[s * seg:(s + 1) * seg, :])
        ya_ref[s * seg:(s + 1) * seg, :] = y.astype(ya_ref.dtype)


def _rnn_branch(proj, conv_w, conv_b, wr, wi, br, bi, lam, h0, *, seq, nseq, nb, row0):
    n = seq * nseq
    seg = n // SUBLANES
    pitch = seg + SUBLANES
    return pl.pallas_call(
        functools.partial(_rnn_kernel, seq=seq, nseq=nseq),
        grid=(nb, RNN_BLOCKS),
        in_specs=[
            pl.BlockSpec((n, RNN_BS), lambda b, k: (row0 + b, k)),
            pl.BlockSpec((n, RNN_BS), lambda b, k: (row0 + b, RNN_BLOCKS + k)),
            pl.BlockSpec((CONV_W, RNN_BS), lambda b, k: (0, k)),
            pl.BlockSpec((1, RNN_BS), lambda b, k: (0, k)),
            pl.BlockSpec((2, None, RNN_BS, RNN_BS), lambda b, k: (0, k, 0, 0)),
            pl.BlockSpec((2, None, RNN_BS, RNN_BS), lambda b, k: (0, k, 0, 0)),
            pl.BlockSpec((2, RNN_BS), lambda b, k: (0, k)),
            pl.BlockSpec((2, RNN_BS), lambda b, k: (0, k)),
            pl.BlockSpec((2, RNN_BS), lambda b, k: (0, k)),
            pl.BlockSpec((nseq, 2, RNN_BS), lambda b, k: (b, 0, k)),
        ],
        out_specs=[
            pl.BlockSpec((n, RNN_BS), lambda b, k: (b, k)),
            pl.BlockSpec((nseq, 2, RNN_BS), lambda b, k: (b, 0, k)),
        ],
        out_shape=[jax.ShapeDtypeStruct((nb * n, D_RNN), BF16), jax.ShapeDtypeStruct((nb * nseq, 2, D_RNN), F32)],
        scratch_shapes=[
            pltpu.VMEM((n + 2 * SUBLANES, RNN_BS), F32),
            pltpu.VMEM((2, SUBLANES * pitch, RNN_BS), F32),
            pltpu.VMEM((2, SUBLANES * pitch, RNN_BS), F32),
        ],
        compiler_params=_cparams(("arbitrary", "arbitrary"), 40),
        name=f"rnn_n{n}",
    )(proj, proj, conv_w, conv_b, wr, wi, br, bi, lam, h0)


def _rope(x, cos, sin_signed):
    lane = lax.broadcasted_iota(jnp.int32, x.shape, 1)
    partner = jnp.where((lane % HD) < HD // 2, pltpu.roll(x, LANES - HD // 2, 1), pltpu.roll(x, HD // 2, 1))
    return x * cos + partner * sin_signed


ATTN_SUB = 128


def _attn_kernel(*refs, n, npast, tq, rope):
    if rope:
        (lq1, lk1, lq2, lk2, q_ref, k_ref, v_ref, gsub_ref, ck_ref, cv_ref, cosk_ref, sink_ref, cosq_ref, sinq_ref,
         o_ref, kb_ref, vb_ref, s_ref, p_ref) = refs
    else:
        lq1, lk1, lq2, lk2, q_ref, k_ref, v_ref, gsub_ref, o_ref, kb_ref, vb_ref, s_ref, p_ref = refs
    chunk = 256
    nk = n + npast

    @pl.when(pl.program_id(2) == 0)
    def _():
        ones_col = jnp.where(lax.broadcasted_iota(jnp.int32, (chunk, VD), 1) == 0, 1.0, 0.0).astype(BF16)
        if npast:
            for r0 in range(0, npast, chunk):
                kb_ref[:, r0:r0 + chunk] = ck_ref[r0:r0 + chunk, :].T.astype(BF16)
                vb_ref[r0:r0 + chunk, VD:2 * VD] = ones_col
            vb_ref[0:npast, 0:VD] = cv_ref[...].astype(BF16)

        for r0 in range(0, n, chunk):
            k = k_ref[r0:r0 + chunk, :]
            if rope:
                k = _rope(k, cosk_ref[r0:r0 + chunk, :], sink_ref[r0:r0 + chunk, :])
            kb_ref[:, npast + r0:npast + r0 + chunk] = k.T.astype(BF16)
            vb_ref[npast + r0:npast + r0 + chunk, 0:VD] = v_ref[r0:r0 + chunk, :].astype(BF16)
            vb_ref[npast + r0:npast + r0 + chunk, VD:2 * VD] = ones_col

    lam = (jnp.exp(jnp.sum(lq1[...] * lk1[...], axis=-1, keepdims=True))
           - jnp.exp(jnp.sum(lq2[...] * lk2[...], axis=-1, keepdims=True)) + LAM_INIT)
    q = q_ref[...]
    if rope:
        q = _rope(q, cosq_ref[...], sinq_ref[...])
    q = q * (HD ** -0.5 * math.log2(math.e))
    lane = lax.broadcasted_iota(jnp.int32, (ATTN_SUB, VD), 1)
    nt = (((1,), (1,)), ((), ()))
    kc = 2 * LANES
    key_chunks = [(c0, min(kc, nk - c0)) for c0 in range(0, nk, kc)]
    for sub in range(tq // ATTN_SUB):
        qs = q[sub * ATTN_SUB:(sub + 1) * ATTN_SUB, :]
        outs = []
        for comp in range(2):
            unit = 2 * sub + comp
            qc = jnp.where((lane < HD) == (comp == 0), qs, 0.0).astype(BF16)
            m_run = None
            for c0, w in key_chunks:
                s = jnp.dot(qc, kb_ref[:, c0:c0 + w], preferred_element_type=F32)
                s_ref[unit, :, c0:c0 + w] = s
                for l0 in range(0, w, LANES):
                    piece = s[:, l0:l0 + LANES]
                    m_run = piece if m_run is None else jnp.maximum(m_run, piece)
            m = jnp.max(m_run, axis=-1, keepdims=True)
            for c0, w in key_chunks:
                p_ref[unit, :, c0:c0 + w] = jnp.exp2(s_ref[unit, :, c0:c0 + w] - m).astype(BF16)
            outs.append(jnp.dot(p_ref[unit], vb_ref[...], preferred_element_type=F32))
        o = (outs[0][:, 0:VD] * (1.0 / outs[0][:, VD:VD + 1])
             - outs[1][:, 0:VD] * (lam / outs[1][:, VD:VD + 1]))
        o = _rms(o, gsub_ref[...]) * (1.0 - LAM_INIT)
        o_ref[sub * ATTN_SUB:(sub + 1) * ATTN_SUB, :] = o.astype(o_ref.dtype)


def _attention(proj, lam_vecs, g_sub, *, n, nb, row0, tq, cache=None, tables=None):
    rope = tables is not None
    npast = 0 if cache is None else cache[0].shape[1]
    nk = n + npast
    qpr = n // tq
    qcol, kcol, vcol = 2 * RNN_BLOCKS, 2 * RNN_BLOCKS + N_HEADS, 2 * RNN_BLOCKS + 2 * N_HEADS
    vec = pl.BlockSpec((1, HD), lambda b, h, t: (0, 0))
    in_specs = [vec, vec, vec, vec,
                pl.BlockSpec((tq, VD), lambda b, h, t: ((row0 + b) * qpr + t, qcol + h)),
                pl.BlockSpec((n, VD), lambda b, h, t: (row0 + b, kcol + h)),
                pl.BlockSpec((n, VD), lambda b, h, t: (row0 + b, vcol + h)),
                pl.BlockSpec((1, VD), lambda b, h, t: (0, 0))]
    args = [*lam_vecs, proj, proj, proj, g_sub]
    if rope:
        ck, cv = cache
        cos_t, sin_t = tables
        in_specs += [pl.BlockSpec((None, npast, VD), lambda b, h, t: (b, 0, h)),
                     pl.BlockSpec((None, npast, VD), lambda b, h, t: (b, 0, h)),
                     pl.BlockSpec((n, VD), lambda b, h, t: (0, 0), pipeline_mode=pl.Buffered(1)),
                     pl.BlockSpec((n, VD), lambda b, h, t: (0, 0), pipeline_mode=pl.Buffered(1)),
                     pl.BlockSpec((tq, VD), lambda b, h, t: (t, 0)),
                     pl.BlockSpec((tq, VD), lambda b, h, t: (t, 0))]
        args += [ck, cv, cos_t, sin_t, cos_t, sin_t]
    return pl.pallas_call(
        functools.partial(_attn_kernel, n=n, npast=npast, tq=tq, rope=rope),
        grid=(nb, N_HEADS, qpr),
        in_specs=in_specs,
        out_specs=pl.BlockSpec((tq, VD), lambda b, h, t: (b * qpr + t, h)),
        out_shape=jax.ShapeDtypeStruct((nb * n, N_HEADS * VD), BF16),
        scratch_shapes=[pltpu.VMEM((VD, nk), BF16), pltpu.VMEM((nk, 2 * VD), BF16),
                        pltpu.VMEM((2 * tq // ATTN_SUB, ATTN_SUB, nk), F32),
                        pltpu.VMEM((2 * tq // ATTN_SUB, ATTN_SUB, nk), BF16)],
        compiler_params=_cparams(("arbitrary", "arbitrary", "arbitrary"), 56),
        name=f"attn_n{n}",
    )(*args)


def _post_kernel(yac_ref, yal_ref, onc_ref, onl_ref, ga0_ref, ga1_ref, gb0_ref, gb1_ref, x_ref, g1_ref, sh2_ref,
                 sc2_ref, gpost1_ref, gpre2_ref, wa_ref, wb_ref, wo_ref, wrt_ref, x1_ref, h2_ref, aff_ref, m_ref,
                 *, ctx_tiles):
    is_ctx = pl.program_id(0) < ctx_tiles
    ya = jnp.where(is_ctx, yac_ref[...], yal_ref[...])
    on = jnp.where(is_ctx, onc_ref[...], onl_ref[...])
    oa = jnp.dot(ya, wa_ref[...], preferred_element_type=F32)
    ob = jnp.dot(on, wb_ref[...], preferred_element_type=F32)
    half = D_MODEL // 2
    m_ref[:, 0:half] = (jax.nn.sigmoid(ga0_ref[...]) * oa[:, 0:half]
                        + jax.nn.sigmoid(gb0_ref[...]) * ob[:, 0:half]).astype(BF16)
    m_ref[:, half:] = (jax.nn.sigmoid(ga1_ref[...]) * oa[:, half:]
                       + jax.nn.sigmoid(gb1_ref[...]) * ob[:, half:]).astype(BF16)
    t = jnp.dot(m_ref[...], wo_ref[...], preferred_element_type=F32)
    x1 = x_ref[...] + g1_ref[...] * _rms(t, gpost1_ref[...])
    x1_ref[...] = x1
    h2 = _rms(x1, gpre2_ref[...]) * (1.0 + sc2_ref[...]) + sh2_ref[...]
    h2_ref[...] = h2
    nt = functools.partial(lax.dot_general, dimension_numbers=(((1,), (1,)), ((), ())), preferred_element_type=F32)
    w_hi, w_lo = _split_bf16(wrt_ref[...])
    h_hi, h_lo = _split_bf16(h2)
    logits = nt(w_hi, h_hi) + (nt(w_lo, h_hi) + nt(w_hi, h_lo))
    e = jnp.exp(logits - jnp.max(logits, axis=0, keepdims=True))
    aff_ref[...] = e / jnp.sum(e, axis=0, keepdims=True)


def _post_mix(ya_c, ya_l, on_c, on_l, proj, x_all, mod3, g_post1, g_pre2, wa, wb, wo, wrt):
    T = x_all.shape[0]
    tm = 256
    tpg = GROUP_TOKENS // tm
    nct = ya_c.shape[0] // tm
    ctx_blk = lambda w: pl.BlockSpec((tm, w), lambda i: (jnp.minimum(i, nct - 1), 0))
    lat_blk = lambda w: pl.BlockSpec((tm, w), lambda i: (jnp.maximum(i - nct, 0), 0))
    half = D_MODEL // 2
    gate0 = (2 * D_RNN + 2 * N_HEADS * VD + N_HEADS * VD) // half
    row = lambda c: pl.BlockSpec((None, 1, D_MODEL), lambda i: (i // tpg, 0, c))
    const = lambda shape: pl.BlockSpec(shape, lambda i: (0,) * len(shape), pipeline_mode=pl.Buffered(1))
    return pl.pallas_call(
        functools.partial(_post_kernel, ctx_tiles=nct),
        grid=(T // tm,),
        in_specs=[
            ctx_blk(D_RNN), lat_blk(D_RNN), ctx_blk(N_HEADS * VD), lat_blk(N_HEADS * VD),
            pl.BlockSpec((tm, half), lambda i: (i, gate0)),
            pl.BlockSpec((tm, half), lambda i: (i, gate0 + 1)),
            pl.BlockSpec((tm, half), lambda i: (i, gate0 + 2)),
            pl.BlockSpec((tm, half), lambda i: (i, gate0 + 3)),
            pl.BlockSpec((tm, D_MODEL), lambda i: (i, 0)),
            row(2), row(3), row(4),
            const((1, D_MODEL)), const((1, D_MODEL)),
            const((D_RNN, D_MODEL)), const((N_HEADS * VD, D_MODEL)), const((D_MODEL, D_MODEL)),
            const((N_EXPERTS, D_MODEL)),
        ],
        out_specs=[
            pl.BlockSpec((tm, D_MODEL), lambda i: (i, 0)),
            pl.BlockSpec((tm, D_MODEL), lambda i: (i, 0)),
            pl.BlockSpec((N_EXPERTS, tm), lambda i: (0, i)),
        ],
        out_shape=[jax.ShapeDtypeStruct((T, D_MODEL), F32), jax.ShapeDtypeStruct((T, D_MODEL), F32),
                   jax.ShapeDtypeStruct((N_EXPERTS, T), F32)],
        scratch_shapes=[pltpu.VMEM((tm, D_MODEL), BF16)],
        compiler_params=_cparams(("arbitrary",), 56),
        name="post_mix",
    )(ya_c, ya_l, on_c, on_l, proj, proj, proj, proj, x_all, mod3, mod3, mod3, g_post1, g_pre2, wa, wb, wo, wrt)


def _cumsum_excl(x, tri):
    n = x.shape[1]
    blk = tri.shape[0]
    outs = []
    carry = jnp.zeros((x.shape[0], 1), F32)
    for c in range(n // blk):
        xc = x[:, c * blk:(c + 1) * blk]
        outs.append(jnp.dot(xc.astype(BF16), tri, preferred_element_type=F32) + carry)
        carry = carry + jnp.sum(xc, axis=1, keepdims=True)
    return outs[0] if len(outs) == 1 else jnp.concatenate(outs, axis=1)


def _route_kernel(aff_ref, idx_ref, val_ref, growt_ref, *, n, cap, row_stride):
    aff = aff_ref[...]

    def bisect(_, lohi):
        lo, hi = lohi
        mid = lo + ((hi - lo) >> 1)
        cnt = jnp.sum((aff >= pltpu.bitcast(mid, F32)).astype(jnp.int32), axis=1, keepdims=True)
        ok = cnt >= cap
        return jnp.where(ok, mid, lo), jnp.where(ok, hi, mid)

    lo0 = jnp.zeros((N_EXPERTS, 1), jnp.int32)
    hi0 = jnp.full((N_EXPERTS, 1), 0x7F800000, jnp.int32)
    thr, _ = lax.fori_loop(0, 31, bisect, (lo0, hi0))
    gt = aff >= pltpu.bitcast(thr + 1, F32)
    eq = (aff >= pltpu.bitcast(thr, F32)) & jnp.logical_not(gt)
    need = cap - jnp.sum(gt.astype(jnp.int32), axis=1, keepdims=True)
    blk = min(n, 256)
    tri = (lax.broadcasted_iota(jnp.int32, (blk, blk), 0) < lax.broadcasted_iota(jnp.int32, (blk, blk), 1)).astype(BF16)
    eqpos = _cumsum_excl(eq.astype(F32), tri)
    sel = gt | (eq & (eqpos < need.astype(F32)))
    pos = _cumsum_excl(sel.astype(F32), tri).astype(jnp.int32)
    slot = jnp.where(sel, pos, -1)
    grow = jnp.where(sel, pos + pl.program_id(0) * row_stride, -1)
    growt_ref[...] = grow.astype(F32).T.astype(jnp.int32)

    rc = min(cap, 32)
    tok = lax.broadcasted_iota(jnp.int32, (rc, n), 1)
    for e in range(N_EXPERTS):
        slot_e = slot[e:e + 1, :]
        aff_e = aff[e:e + 1, :]

        def extract(c, carry):
            r0 = pl.multiple_of(c * rc, rc)
            hit = slot_e == (lax.broadcasted_iota(jnp.int32, (rc, n), 0) + r0)
            idx_ref[e, pl.ds(r0, rc), :] = jnp.sum(jnp.where(hit, tok, 0), axis=1, keepdims=True)
            val_ref[e, pl.ds(r0, rc), :] = jnp.sum(jnp.where(hit, aff_e, 0.0), axis=1, keepdims=True)
            return carry

        lax.fori_loop(0, cap // rc, extract, 0)


def _route(aff_t, *, n, nsets, col0, row_stride):
    cap = EC_FACTOR * n // N_EXPERTS
    kern = functools.partial(_route_kernel, n=n, cap=cap, row_stride=row_stride)
    return pl.pallas_call(
        kern,
        grid=(nsets,),
        in_specs=[pl.BlockSpec((N_EXPERTS, n), lambda s: (0, col0 + s))],
        out_specs=[
            pl.BlockSpec((None, N_EXPERTS, cap, 1), lambda s: (s, 0, 0, 0)),
            pl.BlockSpec((None, N_EXPERTS, cap, 1), lambda s: (s, 0, 0, 0)),
            pl.BlockSpec((n, N_EXPERTS), lambda s: (s, 0)),
        ],
        out_shape=[jax.ShapeDtypeStruct((nsets, N_EXPERTS, cap, 1), jnp.int32),
                   jax.ShapeDtypeStruct((nsets, N_EXPERTS, cap, 1), F32),
                   jax.ShapeDtypeStruct((nsets * n, N_EXPERTS), jnp.int32)],
        compiler_params=_cparams(("arbitrary",), 48),
        name=f"route_n{n}",
    )(aff_t)


def _expert_kernel(cur_ref, nxt_ref, h2_hbm, val_ref, wg_ref, wu_ref, wd_ref, ye_ref, xs_ref, sem, *, rows):
    step = pl.program_id(0) * pl.num_programs(1) + pl.program_id(1)
    nsteps = pl.num_programs(0) * pl.num_programs(1)
    slot = step % 2

    def gather(idx_ref, s):
        def issue(r, carry):
            pltpu.make_async_copy(h2_hbm.at[pl.ds(idx_ref[0, r], 1), :], xs_ref.at[s, pl.ds(r, 1), :],
                                  sem.at[s]).start()
            return carry

        lax.fori_loop(0, rows, issue, 0, unroll=8)

    @pl.when(step == 0)
    def _():
        gather(cur_ref, 0)

    @pl.when(step + 1 < nsteps)
    def _():
        gather(nxt_ref, 1 - slot)

    pltpu.make_async_copy(h2_hbm.at[pl.ds(0, rows), :], xs_ref.at[slot], sem.at[slot]).wait()
    half = rows // 2
    for c in range(2):
        xs = xs_ref[slot, c * half:(c + 1) * half, :].astype(BF16)
        g = jnp.dot(xs, wg_ref[...], preferred_element_type=F32)
        u = jnp.dot(xs, wu_ref[...], preferred_element_type=F32)
        hid = (g * jax.nn.sigmoid(g)) * u
        y = jnp.dot(hid.astype(BF16), wd_ref[...], preferred_element_type=F32) * val_ref[c * half:(c + 1) * half, :]
        ye_ref[c * half:(c + 1) * half, :] = y.astype(ye_ref.dtype)


def _experts(gidx, h2, vals, wg, wu, wd):
    rows_per_expert = vals.shape[1]
    rows = GROUP_ROWS
    groups = rows_per_expert // rows
    last = N_EXPERTS * groups - 1
    return pl.pallas_call(
        functools.partial(_expert_kernel, rows=rows),
        grid=(N_EXPERTS, groups),
        in_specs=[
            pl.BlockSpec((None, 1, rows), lambda e, g: (e * groups + g, 0, 0), memory_space=pltpu.SMEM),
            pl.BlockSpec((None, 1, rows), lambda e, g: (jnp.minimum(e * groups + g + 1, last), 0, 0),
                         memory_space=pltpu.SMEM),
            pl.BlockSpec(memory_space=pl.ANY),
            pl.BlockSpec((None, rows, 1), lambda e, g: (e, g, 0)),
            pl.BlockSpec((None, D_MODEL, D_EXPERT), lambda e, g: (e, 0, 0)),
            pl.BlockSpec((None, D_MODEL, D_EXPERT), lambda e, g: (e, 0, 0)),
            pl.BlockSpec((None, D_EXPERT, D_MODEL), lambda e, g: (e, 0, 0)),
        ],
        out_specs=pl.BlockSpec((None, rows, D_MODEL), lambda e, g: (e, g, 0)),
        out_shape=jax.ShapeDtypeStruct((N_EXPERTS, rows_per_expert, D_MODEL), BF16),
        scratch_shapes=[pltpu.VMEM((2, rows, D_MODEL), F32), pltpu.SemaphoreType.DMA((2,))],
        compiler_params=_cparams(("arbitrary", "arbitrary"), 56),
        name="experts",
    )(gidx.reshape(N_EXPERTS * groups, 1, rows), gidx.reshape(N_EXPERTS * groups, 1, rows), h2, vals, wg, wu, wd)


COMBINE_TM = 256
COMBINE_WIN = 128
BF16_ROW_TILE = 16


def _combine_kernel(win_ref, fast_ref, bounds_ref, growt_ref, ye_hbm, x1_ref, g2_ref, gpost2_ref, o_ref,
                    wbuf_ref, oh_ref, acc_ref, sbuf_ref, wsem, ssem):
    i = pl.program_id(0)
    nsteps = pl.num_programs(0)
    tpg = GROUP_TOKENS // COMBINE_TM
    slot = i % 2
    half = GROUP_ROWS // 2
    growt = growt_ref[...]

    def window_copy(step, e, s):
        start = pl.multiple_of((step // tpg) * GROUP_ROWS + win_ref[step * N_EXPERTS + e], BF16_ROW_TILE)
        return pltpu.make_async_copy(ye_hbm.at[e, pl.ds(start, COMBINE_WIN), :],
                                     wbuf_ref.at[s, pl.ds(e * COMBINE_WIN, COMBINE_WIN), :], wsem.at[s])

    def fetch(step, s):
        for e in range(N_EXPERTS):
            window_copy(step, e, s).start()

    @pl.when((i == 0) & (fast_ref[0] == 1))
    def _():
        fetch(0, 0)

    nxt = jnp.minimum(i + 1, nsteps - 1)

    @pl.when((i + 1 < nsteps) & (fast_ref[nxt] == 1))
    def _():
        fetch(nxt, 1 - slot)

    @pl.when(fast_ref[i] == 1)
    def _():
        for e in range(N_EXPERTS):
            window_copy(i, e, slot).wait()
        for e in range(N_EXPERTS):
            rows = win_ref[i * N_EXPERTS + e] + lax.broadcasted_iota(jnp.int32, (COMBINE_TM, COMBINE_WIN), 1)
            oh_ref[:, e * COMBINE_WIN:(e + 1) * COMBINE_WIN] = jnp.where(growt[:, e:e + 1] == rows, 1.0, 0.0).astype(BF16)
        acc_ref[...] = jnp.dot(oh_ref[...], wbuf_ref[slot], preferred_element_type=F32)

    @pl.when(fast_ref[i] == 0)
    def _():
        acc_ref[...] = jnp.zeros_like(acc_ref)
        for e in range(N_EXPERTS):
            lo = bounds_ref[(i * N_EXPERTS + e) * 2]
            hi = bounds_ref[(i * N_EXPERTS + e) * 2 + 1]
            for c in range(2):
                @pl.when((lo < (c + 1) * half) & (hi > c * half))
                def _():
                    cp = pltpu.make_async_copy(ye_hbm.at[e, pl.ds((i // tpg) * GROUP_ROWS + c * half, half), :],
                                               sbuf_ref, ssem)
                    cp.start()
                    cp.wait()
                    rows = lax.broadcasted_iota(jnp.int32, (COMBINE_TM, half), 1) + c * half
                    onehot = jnp.where(growt[:, e:e + 1] == rows, 1.0, 0.0).astype(BF16)
                    acc_ref[...] += jnp.dot(onehot, sbuf_ref[...], preferred_element_type=F32)

    o_ref[...] = x1_ref[...] + g2_ref[...] * _rms(acc_ref[...], gpost2_ref[...])


def _combine(growt, ye, x1, mod3, g_post2):
    T = x1.shape[0]
    tm = COMBINE_TM
    tpg = GROUP_TOKENS // tm
    g = growt.reshape(T // tm, tm, N_EXPERTS)
    lo = jnp.min(jnp.where(g >= 0, g, GROUP_ROWS), axis=1)
    hi = jnp.max(g, axis=1) + 1
    win = jnp.clip((lo // BF16_ROW_TILE) * BF16_ROW_TILE, 0, GROUP_ROWS - COMBINE_WIN)
    fast = jnp.all((hi <= lo) | (hi <= win + COMBINE_WIN), axis=1)
    bounds = jnp.stack([lo, hi], axis=-1)
    i32 = lambda a: a.reshape(-1).astype(jnp.int32)
    return pl.pallas_call(
        _combine_kernel,
        grid_spec=pltpu.PrefetchScalarGridSpec(
            num_scalar_prefetch=3,
            grid=(T // tm,),
            in_specs=[
                pl.BlockSpec((tm, N_EXPERTS), lambda i, *_: (i, 0)),
                pl.BlockSpec(memory_space=pl.ANY),
                pl.BlockSpec((tm, D_MODEL), lambda i, *_: (i, 0)),
                pl.BlockSpec((None, 1, D_MODEL), lambda i, *_: (i // tpg, 0, 5)),
                pl.BlockSpec((1, D_MODEL), lambda i, *_: (0, 0)),
            ],
            out_specs=pl.BlockSpec((tm, D_MODEL), lambda i, *_: (i, 0)),
            scratch_shapes=[
                pltpu.VMEM((2, N_EXPERTS * COMBINE_WIN, D_MODEL), BF16),
                pltpu.VMEM((tm, N_EXPERTS * COMBINE_WIN), BF16),
                pltpu.VMEM((tm, D_MODEL), F32),
                pltpu.VMEM((GROUP_ROWS // 2, D_MODEL), BF16),
                pltpu.SemaphoreType.DMA((2,)),
                pltpu.SemaphoreType.DMA(()),
            ],
        ),
        out_shape=jax.ShapeDtypeStruct((T, D_MODEL), F32),
        compiler_params=_cparams(("arbitrary",), 48),
        name="combine",
    )(i32(win), i32(fast), i32(bounds), growt, ye, x1, mod3, g_post2)


def _rope_tables(n):
    rows = n // GRID_W
    row = jnp.repeat(jnp.arange(rows), GRID_W).astype(F32)
    col = jnp.tile(jnp.arange(GRID_W), rows).astype(F32)
    inv = ROPE_BASE ** (-jnp.arange(0, HD // 2, 2, dtype=F32) / (HD // 2))
    ang = jnp.concatenate([row[:, None] * inv, col[:, None] * inv], axis=-1)
    cos, sin = jnp.cos(ang), jnp.sin(ang)
    return jnp.tile(cos, (1, 4)), jnp.tile(jnp.concatenate([-sin, sin], axis=-1), (1, 2))


def kernel(x_prompt, x_sample, cache_k, cache_v, state_rnn, c, c_ctx, w_mod, b_mod, g_pre1, g_post1, g_pre2, g_post2, w_in, conv_w, conv_b, lru_wr, lru_br, lru_wi, lru_bi, lru_lam, lam_q1, lam_k1, lam_q2, lam_k2, g_sub, w_a_out, w_b_out, w_o, w_router, w_e_gate, w_e_up, w_e_down):
    nbc, nc, _ = x_prompt.shape
    nbl, nl, _ = x_sample.shape
    assert nbc * nc == GROUP_TOKENS and nl == GROUP_TOKENS and w_mod.shape[0] == 1
    tc = nbc * nc
    x_all = jnp.concatenate([x_prompt.reshape(tc, D_MODEL), x_sample.reshape(nbl * nl, D_MODEL)], axis=0)

    cvec = jnp.concatenate([c_ctx[None, :], c, jnp.zeros((SUBLANES - 1 - nbl, D_MODEL), F32)], axis=0)
    mod = _modulation(cvec, w_mod[0], b_mod)
    mod3 = mod.reshape(SUBLANES, 1, 6 * D_MODEL)

    proj = _in_proj(x_all, mod3, g_pre1, w_in[0].astype(BF16))

    rnn_w = (conv_w[0], conv_b, lru_wr[0], lru_wi[0], lru_br[0], lru_bi[0], lru_lam[0])
    ya_c, h_fin = _rnn_branch(proj, *rnn_w, jnp.zeros((nbc, 2, D_RNN), F32), seq=nc, nseq=SUBLANES,
                              nb=nbc // SUBLANES, row0=0)
    ya_l, _ = _rnn_branch(proj, *rnn_w, state_rnn[:, 0], seq=nl, nseq=1, nb=nbl, row0=tc // nl)

    lam_vecs = (lam_q1, lam_k1, lam_q2, lam_k2)
    on_c = _attention(proj, lam_vecs, g_sub, n=nc, nb=nbc, row0=0, tq=nc)
    past = cache_k.shape[2]
    on_l = _attention(proj, lam_vecs, g_sub, n=nl, nb=nbl, row0=tc // nl, tq=512,
                      cache=(cache_k[:, 0].reshape(nbl, past, N_HEADS * VD),
                             cache_v[:, 0].reshape(nbl, past, N_HEADS * VD)),
                      tables=_rope_tables(nl))

    x1, h2, aff_t = _post_mix(ya_c, ya_l, on_c, on_l, proj, x_all, mod3, g_post1, g_pre2, w_a_out[0].astype(BF16),
                              w_b_out[0].astype(BF16), w_o[0].astype(BF16), w_router[0].T)

    capc = EC_FACTOR * nc // N_EXPERTS
    idx_c, val_c, growt_c = _route(aff_t, n=nc, nsets=nbc, col0=0, row_stride=capc)
    idx_l, val_l, growt_l = _route(aff_t, n=nl, nsets=nbl, col0=tc // nl, row_stride=0)
    gid_c = idx_c[..., 0] + (jnp.arange(nbc, dtype=jnp.int32) * nc)[:, None, None]
    gid_l = idx_l[..., 0] + (tc + jnp.arange(nbl, dtype=jnp.int32) * nl)[:, None, None]
    gidx = jnp.concatenate([gid_c.transpose(1, 0, 2).reshape(N_EXPERTS, -1),
                            gid_l.transpose(1, 0, 2).reshape(N_EXPERTS, -1)], axis=1)
    vals = jnp.concatenate([val_c.transpose(1, 0, 2, 3).reshape(N_EXPERTS, -1, 1),
                            val_l.transpose(1, 0, 2, 3).reshape(N_EXPERTS, -1, 1)], axis=1)
    growt = jnp.concatenate([growt_c, growt_l], axis=0)

    ye = _experts(gidx, h2, vals, w_e_gate[0].astype(BF16), w_e_up[0].astype(BF16),
                  w_e_down[0].astype(BF16))
    x2 = _combine(growt, ye, x1, mod3, g_post2)

    y_prompt = x2[:tc].reshape(nbc, nc, D_MODEL)
    y_sample = x2[tc:].reshape(nbl, nl, D_MODEL)
    kcol = (2 * RNN_BLOCKS + N_HEADS) * VD
    vcol = kcol + N_HEADS * VD
    new_k = proj[:tc, kcol:kcol + N_HEADS * VD].reshape(nbc, 1, nc, N_HEADS, 2 * HD)
    new_v = proj[:tc, vcol:vcol + N_HEADS * VD].reshape(nbc, 1, nc, N_HEADS, VD)
    return (y_prompt, y_sample, new_k, new_v, h_fin[:, None])
```

```python
import functools
import math

import jax
import jax.numpy as jnp
from jax import lax
from jax.experimental import pallas as pl
from jax.experimental.pallas import tpu as pltpu

F32 = jnp.float32
BF16 = jnp.bfloat16
HIGHEST = lax.Precision.HIGHEST

D_MODEL = 2048
D_RNN = 1024
RNN_BLOCKS = 8
RNN_BS = 128
CONV_W = 4
LRU_C = 8.0
N_HEADS = 8
HD = 64
VD = 128
N_EXPERTS = 16
D_EXPERT = 1024
EC_FACTOR = 2
EPS = 1e-6
IN_COLS = 9216
GRID_W = 64
ROPE_BASE = 10000.0
LAM_INIT = 0.8 - 0.6 * math.exp(-0.3 * 0)

GROUP_TOKENS = 4096
GROUP_ROWS = 512
SUBLANES = 8
LANES = 128
V7X_VMEM_BYTES = 64 * 1024 * 1024


def _cparams(semantics, vmem_mb):
    return pltpu.CompilerParams(dimension_semantics=semantics, vmem_limit_bytes=vmem_mb * 1024 * 1024)


def _sigmoid(x):
    return 0.5 * jnp.tanh(0.5 * x) + 0.5


def _rms(x, g):
    return x * lax.rsqrt(jnp.mean(x * x, axis=-1, keepdims=True) + EPS) * g


def _mod_kernel(c_ref, w_ref, b_ref, o_ref):
    c = c_ref[...]
    s = c * _sigmoid(c)
    o_ref[...] = jnp.dot(s, w_ref[...], precision=HIGHEST, preferred_element_type=F32) + b_ref[...]


def _modulation(cvec, w_mod, b_mod):
    tn = 1024
    return pl.pallas_call(
        _mod_kernel,
        grid=(6 * D_MODEL // tn,),
        in_specs=[
            pl.BlockSpec((SUBLANES, D_MODEL), lambda j: (0, 0)),
            pl.BlockSpec((D_MODEL, tn), lambda j: (0, j)),
            pl.BlockSpec((1, tn), lambda j: (0, j)),
        ],
        out_specs=pl.BlockSpec((SUBLANES, tn), lambda j: (0, j)),
        out_shape=jax.ShapeDtypeStruct((SUBLANES, 6 * D_MODEL), F32),
        compiler_params=_cparams(("arbitrary",), 40),
        name="modulation",
    )(cvec, w_mod, b_mod)


def _in_kernel(xc_ref, xl_ref, sh_ref, sc_ref, g_ref, w_ref, o_ref, h_ref, *, ctx_tiles):
    i, j = pl.program_id(0), pl.program_id(1)
    rows = 128

    def prologue(x_ref):
        def body(c, carry):
            r0 = pl.multiple_of(c * rows, rows)
            h = _rms(x_ref[pl.ds(r0, rows), :], g_ref[...]) * (1.0 + sc_ref[...]) + sh_ref[...]
            h_ref[pl.ds(r0, rows), :] = h.astype(BF16)
            return carry

        lax.fori_loop(0, h_ref.shape[0] // rows, body, 0)

    @pl.when((j == 0) & (i < ctx_tiles))
    def _():
        prologue(xc_ref)

    @pl.when((j == 0) & (i >= ctx_tiles))
    def _():
        prologue(xl_ref)

    o_ref[...] = jnp.dot(h_ref[...], w_ref[...], preferred_element_type=F32)


def _in_proj(x_c, x_l, mod3, g_pre1, w_in_bf):
    T = x_c.shape[0] + x_l.shape[0]
    tm, tn = 1024, 1024
    tpg = GROUP_TOKENS // tm
    nct = x_c.shape[0] // tm
    return pl.pallas_call(
        functools.partial(_in_kernel, ctx_tiles=nct),
        grid=(T // tm, IN_COLS // tn),
        in_specs=[
            pl.BlockSpec((tm, D_MODEL), lambda i, j: (jnp.minimum(i, nct - 1), 0), pipeline_mode=pl.Buffered(1)),
            pl.BlockSpec((tm, D_MODEL), lambda i, j: (jnp.maximum(i - nct, 0), 0)),
            pl.BlockSpec((None, 1, D_MODEL), lambda i, j: (i // tpg, 0, 0)),
            pl.BlockSpec((None, 1, D_MODEL), lambda i, j: (i // tpg, 0, 1)),
            pl.BlockSpec((1, D_MODEL), lambda i, j: (0, 0)),
            pl.BlockSpec((D_MODEL, tn), lambda i, j: (0, j)),
        ],
        out_specs=pl.BlockSpec((tm, tn), lambda i, j: (i, j)),
        out_shape=jax.ShapeDtypeStruct((T, IN_COLS), F32),
        scratch_shapes=[pltpu.VMEM((tm, D_MODEL), BF16)],
        compiler_params=_cparams(("arbitrary", "arbitrary"), 56),
        name="in_proj",
    )(x_c, x_l, mod3, mod3, g_pre1, w_in_bf)


def _copy_kernel(x_ref, o_ref):
    o_ref[...] = x_ref[...]


def _take_cols(proj, rows, col_block, width):
    tm = 1024
    return pl.pallas_call(
        _copy_kernel,
        grid=(rows // tm,),
        in_specs=[pl.BlockSpec((tm, width), lambda i: (i, col_block))],
        out_specs=pl.BlockSpec((tm, width), lambda i: (i, 0)),
        out_shape=jax.ShapeDtypeStruct((rows, width), proj.dtype),
        compiler_params=_cparams(("arbitrary",), 40),
        name="take_cols",
    )(proj)


def _gelu_tanh(x):
    return 0.5 * x * (1.0 + jnp.tanh(math.sqrt(2.0 / math.pi) * (x + 0.044715 * (x * x * x))))


def _softplus(x):
    return jnp.maximum(x, 0.0) + jnp.log(1.0 + jnp.exp(-jnp.abs(x)))


def _split_bf16(x):
    hi = x.astype(BF16)
    return hi, (x - hi.astype(F32)).astype(BF16)


def _dot_3pass(x, w_hi, w_lo):
    x_hi, x_lo = _split_bf16(x)
    d = functools.partial(jnp.dot, preferred_element_type=F32)
    return d(x_hi, w_hi) + (d(x_lo, w_hi) + d(x_hi, w_lo))


def _rnn_kernel(xr_ref, gr_ref, cw_ref, cb_ref, wr_ref, wi_ref, br_ref, bi_ref, lam_ref, h0_ref,
                ya_ref, hf_ref, xp_ref, a_ref, b_ref, h_ref, p_ref, *, seq, nseq):
    n = seq * nseq
    seg = n // SUBLANES
    pitch = seg + SUBLANES
    zeros8 = jnp.zeros((SUBLANES, LANES), F32)
    xp_ref[0:SUBLANES, :] = zeros8
    xp_ref[SUBLANES + n:2 * SUBLANES + n, :] = zeros8
    xp_ref[SUBLANES:SUBLANES + n, :] = xr_ref[...]
    cw = cw_ref[...]
    cb = cb_ref[...]
    sp = _softplus(-lam_ref[...])
    w_r = [_split_bf16(wr_ref[d]) for d in range(2)]
    w_i = [_split_bf16(wi_ref[d]) for d in range(2)]

    def gates(s, carry):
        r0 = pl.multiple_of(s * seg, SUBLANES)
        t = (r0 + lax.broadcasted_iota(jnp.int32, (seg, LANES), 0)) % seq
        xc = cb + jnp.where(t >= 2, xp_ref[pl.ds(r0 + 6, seg), :], 0.0) * cw[0:1]
        xc = xc + jnp.where(t >= 1, xp_ref[pl.ds(r0 + 7, seg), :], 0.0) * cw[1:2]
        xc = xc + xp_ref[pl.ds(r0 + 8, seg), :] * cw[2:3]
        xc = xc + jnp.where(t <= seq - 2, xp_ref[pl.ds(r0 + 9, seg), :], 0.0) * cw[3:4]
        p0 = pl.multiple_of(s * pitch, SUBLANES)
        for d in range(2):
            r = _sigmoid(_dot_3pass(xc, *w_r[d]) + br_ref[d:d + 1, :])
            i = _sigmoid(_dot_3pass(xc, *w_i[d]) + bi_ref[d:d + 1, :])
            a = jnp.exp((-LRU_C) * r * sp[d:d + 1, :])
            y = 1.0 - a * a
            a_ref[d, pl.ds(p0, seg), :] = a
            b_ref[d, pl.ds(p0, seg), :] = jnp.where(y > 0.0, y * lax.rsqrt(y), 0.0) * (i * xc)
        return carry

    lax.fori_loop(0, SUBLANES, gates, 0)

    def scan(j, carry):
        hf, af, hb, ab = carry
        a0 = a_ref[0, pl.ds(j, SUBLANES, stride=pitch), :]
        b0 = b_ref[0, pl.ds(j, SUBLANES, stride=pitch), :]
        hf = a0 * hf + b0
        af = a0 * af
        h_ref[0, pl.ds(j, SUBLANES, stride=pitch), :] = hf
        p_ref[0, pl.ds(j, SUBLANES, stride=pitch), :] = af
        jb = seg - 1 - j
        a1 = a_ref[1, pl.ds(jb, SUBLANES, stride=pitch), :]
        b1 = b_ref[1, pl.ds(jb, SUBLANES, stride=pitch), :]
        hb = a1 * hb + b1
        ab = a1 * ab
        h_ref[1, pl.ds(jb, SUBLANES, stride=pitch), :] = hb
        p_ref[1, pl.ds(jb, SUBLANES, stride=pitch), :] = ab
        return hf, af, hb, ab

    ones8 = jnp.ones((SUBLANES, LANES), F32)
    ef, pf, eb, pb = lax.fori_loop(0, seg, scan, (zeros8, ones8, zeros8, ones8), unroll=8)

    spq = SUBLANES // nseq
    cf = [None] * SUBLANES
    cbk = [None] * SUBLANES
    for q in range(nseq):
        h0 = h0_ref[q]
        first, last = q * spq, q * spq + spq - 1
        cf[first] = h0[0:1, :]
        for s in range(first + 1, last + 1):
            cf[s] = ef[s - 1:s, :] + pf[s - 1:s, :] * cf[s - 1]
        cbk[last] = h0[1:2, :]
        for s in range(last - 1, first - 1, -1):
            cbk[s] = eb[s + 1:s + 2, :] + pb[s + 1:s + 2, :] * cbk[s + 1]
        hf_ref[q, 0:1, :] = ef[last:last + 1, :] + pf[last:last + 1, :] * cf[last]
        hf_ref[q, 1:2, :] = eb[first:first + 1, :] + pb[first:first + 1, :] * cbk[first]

    for s in range(SUBLANES):
        hfw = h_ref[0, s * pitch:s * pitch + seg, :] + p_ref[0, s * pitch:s * pitch + seg, :] * cf[s]
        hbw = h_ref[1, s * pitch:s * pitch + seg, :] + p_ref[1, s * pitch:s * pitch + seg, :] * cbk[s]
        y = (hfw + hbw) * _gelu_tanh(gr_ref[s * seg:(s + 1) * seg, :])
        ya_ref[s * seg:(s + 1) * seg, :] = y.astype(ya_ref.dtype)


def _rnn_branch(proj, conv_w, conv_b, wr, wi, br, bi, lam, h0, *, seq, nseq, nb, row0):
    n = seq * nseq
    seg = n // SUBLANES
    pitch = seg + SUBLANES
    return pl.pallas_call(
        functools.partial(_rnn_kernel, seq=seq, nseq=nseq),
        grid=(nb, RNN_BLOCKS),
        in_specs=[
            pl.BlockSpec((n, RNN_BS), lambda b, k: (row0 + b, k)),
            pl.BlockSpec((n, RNN_BS), lambda b, k: (row0 + b, RNN_BLOCKS + k)),
            pl.BlockSpec((CONV_W, RNN_BS), lambda b, k: (0, k)),
            pl.BlockSpec((1, RNN_BS), lambda b, k: (0, k)),
            pl.BlockSpec((2, None, RNN_BS, RNN_BS), lambda b, k: (0, k, 0, 0)),
            pl.BlockSpec((2, None, RNN_BS, RNN_BS), lambda b, k: (0, k, 0, 0)),
            pl.BlockSpec((2, RNN_BS), lambda b, k: (0, k)),
            pl.BlockSpec((2, RNN_BS), lambda b, k: (0, k)),
            pl.BlockSpec((2, RNN_BS), lambda b, k: (0, k)),
            pl.BlockSpec((nseq, 2, RNN_BS), lambda b, k: (b, 0, k)),
        ],
        out_specs=[
            pl.BlockSpec((n, RNN_BS), lambda b, k: (b, k)),
            pl.BlockSpec((nseq, 2, RNN_BS), lambda b, k: (b, 0, k)),
        ],
        out_shape=[jax.ShapeDtypeStruct((nb * n, D_RNN), BF16), jax.ShapeDtypeStruct((nb * nseq, 2, D_RNN), F32)],
        scratch_shapes=[
            pltpu.VMEM((n + 2 * SUBLANES, RNN_BS), F32),
        ] + [pltpu.VMEM((2, SUBLANES * pitch, RNN_BS), F32)] * 4,
        compiler_params=_cparams(("arbitrary", "arbitrary"), 40),
        name=f"rnn_n{n}",
    )(proj, proj, conv_w, conv_b, wr, wi, br, bi, lam, h0)


def _rope(x, cos, sin_signed):
    lane = lax.broadcasted_iota(jnp.int32, x.shape, 1)
    partner = jnp.where((lane % HD) < HD // 2, pltpu.roll(x, LANES - HD // 2, 1), pltpu.roll(x, HD // 2, 1))
    return x * cos + partner * sin_signed


ATTN_SUB = 128


def _attn_kernel(*refs, n, npast, tq, rope):
    if rope:
        (lq1, lk1, lq2, lk2, q_ref, k_ref, v_ref, gsub_ref, ck_ref, cv_ref, cosk_ref, sink_ref, cosq_ref, sinq_ref,
         o_ref, kb_ref, vb_ref, s_ref, p_ref) = refs
    else:
        lq1, lk1, lq2, lk2, q_ref, k_ref, v_ref, gsub_ref, o_ref, kb_ref, vb_ref, s_ref, p_ref = refs
    chunk = 256
    nk = n + npast

    @pl.when(pl.program_id(2) == 0)
    def _():
        ones_col = jnp.where(lax.broadcasted_iota(jnp.int32, (chunk, VD), 1) == 0, 1.0, 0.0).astype(BF16)
        if npast:
            for r0 in range(0, npast, chunk):
                kb_ref[:, r0:r0 + chunk] = ck_ref[r0:r0 + chunk, :].T.astype(BF16)
                vb_ref[r0:r0 + chunk, VD:2 * VD] = ones_col
            vb_ref[0:npast, 0:VD] = cv_ref[...].astype(BF16)

        for r0 in range(0, n, chunk):
            k = k_ref[r0:r0 + chunk, :]
            if rope:
                k = _rope(k, cosk_ref[r0:r0 + chunk, :], sink_ref[r0:r0 + chunk, :])
            kb_ref[:, npast + r0:npast + r0 + chunk] = k.T.astype(BF16)
            vb_ref[npast + r0:npast + r0 + chunk, 0:VD] = v_ref[r0:r0 + chunk, :].astype(BF16)
            vb_ref[npast + r0:npast + r0 + chunk, VD:2 * VD] = ones_col

    lam = (jnp.exp(jnp.sum(lq1[...] * lk1[...], axis=-1, keepdims=True))
           - jnp.exp(jnp.sum(lq2[...] * lk2[...], axis=-1, keepdims=True)) + LAM_INIT)
    q = q_ref[...]
    if rope:
        q = _rope(q, cosq_ref[...], sinq_ref[...])
    q = q * (HD ** -0.5 * math.log2(math.e))
    lane = lax.broadcasted_iota(jnp.int32, (ATTN_SUB, VD), 1)
    nt = (((1,), (1,)), ((), ()))
    kc = 2 * LANES
    key_chunks = [(c0, min(kc, nk - c0)) for c0 in range(0, nk, kc)]
    for sub in range(tq // ATTN_SUB):
        qs = q[sub * ATTN_SUB:(sub + 1) * ATTN_SUB, :]
        outs = []
        for comp in range(2):
            unit = 2 * sub + comp
            qc = jnp.where((lane < HD) == (comp == 0), qs, 0.0).astype(BF16)
            m_run = None
            for c0, w in key_chunks:
                s = jnp.dot(qc, kb_ref[:, c0:c0 + w], preferred_element_type=F32)
                s_ref[unit, :, c0:c0 + w] = s
                for l0 in range(0, w, LANES):
                    piece = s[:, l0:l0 + LANES]
                    m_run = piece if m_run is None else jnp.maximum(m_run, piece)
            m = jnp.max(m_run, axis=-1, keepdims=True)
            for c0, w in key_chunks:
                p_ref[unit, :, c0:c0 + w] = jnp.exp2(s_ref[unit, :, c0:c0 + w] - m).astype(BF16)
            outs.append(jnp.dot(p_ref[unit], vb_ref[...], preferred_element_type=F32))
        o = (outs[0][:, 0:VD] * (1.0 / outs[0][:, VD:VD + 1])
             - outs[1][:, 0:VD] * (lam / outs[1][:, VD:VD + 1]))
        o = _rms(o, gsub_ref[...]) * (1.0 - LAM_INIT)
        o_ref[sub * ATTN_SUB:(sub + 1) * ATTN_SUB, :] = o.astype(o_ref.dtype)


def _attention(proj, lam_vecs, g_sub, *, n, nb, row0, tq, cache=None, tables=None):
    rope = tables is not None
    npast = 0 if cache is None else cache[0].shape[1]
    nk = n + npast
    qpr = n // tq
    qcol, kcol, vcol = 2 * RNN_BLOCKS, 2 * RNN_BLOCKS + N_HEADS, 2 * RNN_BLOCKS + 2 * N_HEADS
    vec = pl.BlockSpec((1, HD), lambda b, h, t: (0, 0))
    in_specs = [vec, vec, vec, vec,
                pl.BlockSpec((tq, VD), lambda b, h, t: ((row0 + b) * qpr + t, qcol + h)),
                pl.BlockSpec((n, VD), lambda b, h, t: (row0 + b, kcol + h)),
                pl.BlockSpec((n, VD), lambda b, h, t: (row0 + b, vcol + h)),
                pl.BlockSpec((1, VD), lambda b, h, t: (0, 0))]
    args = [*lam_vecs, proj, proj, proj, g_sub]
    if rope:
        ck, cv = cache
        cos_t, sin_t = tables
        in_specs += [pl.BlockSpec((None, npast, VD), lambda b, h, t: (b, 0, h)),
                     pl.BlockSpec((None, npast, VD), lambda b, h, t: (b, 0, h)),
                     pl.BlockSpec((n, VD), lambda b, h, t: (0, 0), pipeline_mode=pl.Buffered(1)),
                     pl.BlockSpec((n, VD), lambda b, h, t: (0, 0), pipeline_mode=pl.Buffered(1)),
                     pl.BlockSpec((tq, VD), lambda b, h, t: (t, 0)),
                     pl.BlockSpec((tq, VD), lambda b, h, t: (t, 0))]
        args += [ck, cv, cos_t, sin_t, cos_t, sin_t]
    return pl.pallas_call(
        functools.partial(_attn_kernel, n=n, npast=npast, tq=tq, rope=rope),
        grid=(nb, N_HEADS, qpr),
        in_specs=in_specs,
        out_specs=pl.BlockSpec((tq, VD), lambda b, h, t: (b * qpr + t, h)),
        out_shape=jax.ShapeDtypeStruct((nb * n, N_HEADS * VD), BF16),
        scratch_shapes=[pltpu.VMEM((VD, nk), BF16), pltpu.VMEM((nk, 2 * VD), BF16),
                        pltpu.VMEM((2 * tq // ATTN_SUB, ATTN_SUB, nk), F32),
                        pltpu.VMEM((2 * tq // ATTN_SUB, ATTN_SUB, nk), BF16)],
        compiler_params=_cparams(("arbitrary", "arbitrary", "arbitrary"), 56),
        name=f"attn_n{n}",
    )(*args)


POST_UNIT = 128


def _post_kernel(yac_ref, yal_ref, onc_ref, onl_ref, ga0_ref, ga1_ref, gb0_ref, gb1_ref, xc_ref, xl_ref, g1_ref, sh2_ref,
                 sc2_ref, gpost1_ref, gpre2_ref, wa_ref, wb_ref, wo_ref, wrt_ref, x1_ref, h2_ref, aff_ref,
                 *, ctx_tiles):
    is_ctx = pl.program_id(0) < ctx_tiles
    half = D_MODEL // 2
    nt = functools.partial(lax.dot_general, dimension_numbers=(((1,), (1,)), ((), ())), preferred_element_type=F32)
    w_hi, w_lo = _split_bf16(wrt_ref[...])
    tm = x1_ref.shape[0]
    for r0 in range(0, tm, POST_UNIT):
        rs = slice(r0, r0 + POST_UNIT)
        ya = jnp.where(is_ctx, yac_ref[rs, :], yal_ref[rs, :])
        on = jnp.where(is_ctx, onc_ref[rs, :], onl_ref[rs, :])
        oa = jnp.dot(ya, wa_ref[...], preferred_element_type=F32)
        ob = jnp.dot(on, wb_ref[...], preferred_element_type=F32)
        m0 = (_sigmoid(ga0_ref[rs, :]) * oa[:, 0:half] + _sigmoid(gb0_ref[rs, :]) * ob[:, 0:half]).astype(BF16)
        m1 = (_sigmoid(ga1_ref[rs, :]) * oa[:, half:] + _sigmoid(gb1_ref[rs, :]) * ob[:, half:]).astype(BF16)
        t = (jnp.dot(m0, wo_ref[0:half, :], preferred_element_type=F32)
             + jnp.dot(m1, wo_ref[half:, :], preferred_element_type=F32))
        x1 = jnp.where(is_ctx, xc_ref[rs, :], xl_ref[rs, :]) + g1_ref[...] * _rms(t, gpost1_ref[...])
        x1_ref[rs, :] = x1
        h2 = _rms(x1, gpre2_ref[...]) * (1.0 + sc2_ref[...]) + sh2_ref[...]
        h2_ref[rs, :] = h2
        h_hi, h_lo = _split_bf16(h2)
        logits = nt(w_hi, h_hi) + (nt(w_lo, h_hi) + nt(w_hi, h_lo))
        e = jnp.exp(logits - jnp.max(logits, axis=0, keepdims=True))
        aff_ref[:, rs] = e / jnp.sum(e, axis=0, keepdims=True)


def _post_mix(ya_c, ya_l, on_c, on_l, proj, x_c, x_l, mod3, g_post1, g_pre2, wa, wb, wo, wrt):
    T = x_c.shape[0] + x_l.shape[0]
    tm = 256
    tpg = GROUP_TOKENS // tm
    nct = ya_c.shape[0] // tm
    ctx_blk = lambda w: pl.BlockSpec((tm, w), lambda i: (jnp.minimum(i, nct - 1), 0))
    lat_blk = lambda w: pl.BlockSpec((tm, w), lambda i: (jnp.maximum(i - nct, 0), 0))
    half = D_MODEL // 2
    gate0 = (2 * D_RNN + 2 * N_HEADS * VD + N_HEADS * VD) // half
    row = lambda c: pl.BlockSpec((None, 1, D_MODEL), lambda i: (i // tpg, 0, c))
    const = lambda shape: pl.BlockSpec(shape, lambda i: (0,) * len(shape), pipeline_mode=pl.Buffered(1))
    return pl.pallas_call(
        functools.partial(_post_kernel, ctx_tiles=nct),
        grid=(T // tm,),
        in_specs=[
            ctx_blk(D_RNN), lat_blk(D_RNN), ctx_blk(N_HEADS * VD), lat_blk(N_HEADS * VD),
            pl.BlockSpec((tm, half), lambda i: (i, gate0)),
            pl.BlockSpec((tm, half), lambda i: (i, gate0 + 1)),
            pl.BlockSpec((tm, half), lambda i: (i, gate0 + 2)),
            pl.BlockSpec((tm, half), lambda i: (i, gate0 + 3)),
            ctx_blk(D_MODEL), lat_blk(D_MODEL),
            row(2), row(3), row(4),
            const((1, D_MODEL)), const((1, D_MODEL)),
            const((D_RNN, D_MODEL)), const((N_HEADS * VD, D_MODEL)), const((D_MODEL, D_MODEL)),
            const((N_EXPERTS, D_MODEL)),
        ],
        out_specs=[
            pl.BlockSpec((tm, D_MODEL), lambda i: (i, 0)),
            pl.BlockSpec((tm, D_MODEL), lambda i: (i, 0)),
            pl.BlockSpec((N_EXPERTS, tm), lambda i: (0, i)),
        ],
        out_shape=[jax.ShapeDtypeStruct((T, D_MODEL), F32), jax.ShapeDtypeStruct((T, D_MODEL), F32),
                   jax.ShapeDtypeStruct((N_EXPERTS, T), F32)],
        compiler_params=_cparams(("arbitrary",), 56),
        name="post_mix",
    )(ya_c, ya_l, on_c, on_l, proj, proj, proj, proj, x_c, x_l, mod3, mod3, mod3, g_post1, g_pre2, wa, wb, wo, wrt)


def _cumsum_excl(x, tri):
    n = x.shape[1]
    blk = tri.shape[0]
    outs = []
    carry = jnp.zeros((x.shape[0], 1), F32)
    for c in range(n // blk):
        xc = x[:, c * blk:(c + 1) * blk]
        outs.append(jnp.dot(xc.astype(BF16), tri, preferred_element_type=F32) + carry)
        carry = carry + jnp.sum(xc, axis=1, keepdims=True)
    return outs[0] if len(outs) == 1 else jnp.concatenate(outs, axis=1)


def _route_kernel(aff_ref, idx_ref, val_ref, growt_ref, *, n, cap, row_stride):
    aff = aff_ref[...]

    def bisect(_, lohi):
        lo, hi = lohi
        mid = lo + ((hi - lo) >> 1)
        cnt = jnp.sum((aff >= pltpu.bitcast(mid, F32)).astype(jnp.int32), axis=1, keepdims=True)
        ok = cnt >= cap
        return jnp.where(ok, mid, lo), jnp.where(ok, hi, mid)

    lo0 = jnp.zeros((N_EXPERTS, 1), jnp.int32)
    hi0 = jnp.full((N_EXPERTS, 1), 0x7F800000, jnp.int32)
    thr, _ = lax.fori_loop(0, 31, bisect, (lo0, hi0))
    gt = aff >= pltpu.bitcast(thr + 1, F32)
    eq = (aff >= pltpu.bitcast(thr, F32)) & jnp.logical_not(gt)
    need = cap - jnp.sum(gt.astype(jnp.int32), axis=1, keepdims=True)
    blk = min(n, 256)
    tri = (lax.broadcasted_iota(jnp.int32, (blk, blk), 0) < lax.broadcasted_iota(jnp.int32, (blk, blk), 1)).astype(BF16)
    eqpos = _cumsum_excl(eq.astype(F32), tri)
    sel = gt | (eq & (eqpos < need.astype(F32)))
    pos = _cumsum_excl(sel.astype(F32), tri).astype(jnp.int32)
    slot = jnp.where(sel, pos, -1)
    grow = jnp.where(sel, pos + pl.program_id(0) * row_stride, -1)
    growt_ref[...] = grow.astype(F32).T.astype(jnp.int32)

    rc = min(cap, 32)
    tok = lax.broadcasted_iota(jnp.int32, (rc, n), 1)
    for e in range(N_EXPERTS):
        slot_e = slot[e:e + 1, :]
        aff_e = aff[e:e + 1, :]

        def extract(c, carry):
            r0 = pl.multiple_of(c * rc, rc)
            hit = slot_e == (lax.broadcasted_iota(jnp.int32, (rc, n), 0) + r0)
            idx_ref[e, pl.ds(r0, rc), :] = jnp.sum(jnp.where(hit, tok, 0), axis=1, keepdims=True)
            val_ref[e, pl.ds(r0, rc), :] = jnp.sum(jnp.where(hit, aff_e, 0.0), axis=1, keepdims=True)
            return carry

        lax.fori_loop(0, cap // rc, extract, 0)


def _route(aff_t, *, n, nsets, col0, row_stride):
    cap = EC_FACTOR * n // N_EXPERTS
    kern = functools.partial(_route_kernel, n=n, cap=cap, row_stride=row_stride)
    return pl.pallas_call(
        kern,
        grid=(nsets,),
        in_specs=[pl.BlockSpec((N_EXPERTS, n), lambda s: (0, col0 + s))],
        out_specs=[
            pl.BlockSpec((None, N_EXPERTS, cap, 1), lambda s: (s, 0, 0, 0)),
            pl.BlockSpec((None, N_EXPERTS, cap, 1), lambda s: (s, 0, 0, 0)),
            pl.BlockSpec((n, N_EXPERTS), lambda s: (s, 0)),
        ],
        out_shape=[jax.ShapeDtypeStruct((nsets, N_EXPERTS, cap, 1), jnp.int32),
                   jax.ShapeDtypeStruct((nsets, N_EXPERTS, cap, 1), F32),
                   jax.ShapeDtypeStruct((nsets * n, N_EXPERTS), jnp.int32)],
        compiler_params=_cparams(("arbitrary",), 48),
        name=f"route_n{n}",
    )(aff_t)


def _expert_kernel(cur_ref, nxt_ref, h2_hbm, val_ref, wg_ref, wu_ref, wd_ref, ye_ref, xs_ref, sem, *, rows):
    step = pl.program_id(0) * pl.num_programs(1) + pl.program_id(1)
    nsteps = pl.num_programs(0) * pl.num_programs(1)
    slot = step % 2

    def row_copy(idx_ref, r, s):
        return pltpu.make_async_copy(h2_hbm.at[pl.ds(idx_ref[0, r], 1), :], xs_ref.at[s, pl.ds(r, 1), :], sem.at[s])

    def slot_wait(s):
        pltpu.make_async_copy(h2_hbm.at[pl.ds(0, rows), :], xs_ref.at[s], sem.at[s]).wait()

    @pl.when(step == 0)
    def _():
        def issue(r, carry):
            row_copy(cur_ref, r, 0).start()
            return carry

        lax.fori_loop(0, rows, issue, 0, unroll=8)

    for r in range(rows):
        row_copy(nxt_ref, r, 1 - slot).start()
    slot_wait(slot)
    half = rows // 2
    for c in range(2):
        xs = xs_ref[slot, c * half:(c + 1) * half, :].astype(BF16)
        g = jnp.dot(xs, wg_ref[...], preferred_element_type=F32)
        u = jnp.dot(xs, wu_ref[...], preferred_element_type=F32)
        hid = (g * _sigmoid(g)) * u
        y = jnp.dot(hid.astype(BF16), wd_ref[...], preferred_element_type=F32) * val_ref[c * half:(c + 1) * half, :]
        ye_ref[c * half:(c + 1) * half, :] = y.astype(ye_ref.dtype)

    @pl.when(step == nsteps - 1)
    def _():
        slot_wait(1 - slot)


def _experts(gidx, h2, vals, wg, wu, wd):
    rows_per_expert = vals.shape[1]
    rows = GROUP_ROWS
    groups = rows_per_expert // rows
    last = N_EXPERTS * groups - 1
    return pl.pallas_call(
        functools.partial(_expert_kernel, rows=rows),
        grid=(N_EXPERTS, groups),
        in_specs=[
            pl.BlockSpec((None, 1, rows), lambda e, g: (e * groups + g, 0, 0), memory_space=pltpu.SMEM),
            pl.BlockSpec((None, 1, rows), lambda e, g: (jnp.minimum(e * groups + g + 1, last), 0, 0),
                         memory_space=pltpu.SMEM),
            pl.BlockSpec(memory_space=pl.ANY),
            pl.BlockSpec((None, rows, 1), lambda e, g: (e, g, 0)),
            pl.BlockSpec((None, D_MODEL, D_EXPERT), lambda e, g: (e, 0, 0)),
            pl.BlockSpec((None, D_MODEL, D_EXPERT), lambda e, g: (e, 0, 0)),
            pl.BlockSpec((None, D_EXPERT, D_MODEL), lambda e, g: (e, 0, 0)),
        ],
        out_specs=pl.BlockSpec((None, rows, D_MODEL), lambda e, g: (e, g, 0)),
        out_shape=jax.ShapeDtypeStruct((N_EXPERTS, rows_per_expert, D_MODEL), BF16),
        scratch_shapes=[pltpu.VMEM((2, rows, D_MODEL), F32), pltpu.SemaphoreType.DMA((2,))],
        compiler_params=_cparams(("arbitrary", "arbitrary"), 56),
        name="experts",
    )(gidx.reshape(N_EXPERTS * groups, 1, rows), gidx.reshape(N_EXPERTS * groups, 1, rows), h2, vals, wg, wu, wd)


COMBINE_TM = 256
COMBINE_WIN = 128
BF16_ROW_TILE = 16


def _combine_kernel(win_ref, fast_ref, bounds_ref, growt_ref, ye_hbm, x1_ref, g2_ref, gpost2_ref, o_ref,
                    wbuf_ref, oh_ref, acc_ref, sbuf_ref, wsem, ssem):
    i = pl.program_id(0)
    nsteps = pl.num_programs(0)
    tpg = GROUP_TOKENS // COMBINE_TM
    slot = i % 2
    half = GROUP_ROWS // 2
    growt = growt_ref[...]

    def window_copy(step, e, s):
        start = pl.multiple_of((step // tpg) * GROUP_ROWS + win_ref[step * N_EXPERTS + e], BF16_ROW_TILE)
        return pltpu.make_async_copy(ye_hbm.at[e, pl.ds(start, COMBINE_WIN), :],
                                     wbuf_ref.at[s, pl.ds(e * COMBINE_WIN, COMBINE_WIN), :], wsem.at[s])

    def fetch(step, s):
        for e in range(N_EXPERTS):
            window_copy(step, e, s).start()

    @pl.when((i == 0) & (fast_ref[0] == 1))
    def _():
        fetch(0, 0)

    nxt = jnp.minimum(i + 1, nsteps - 1)

    @pl.when((i + 1 < nsteps) & (fast_ref[nxt] == 1))
    def _():
        fetch(nxt, 1 - slot)

    @pl.when(fast_ref[i] == 1)
    def _():
        for e in range(N_EXPERTS):
            window_copy(i, e, slot).wait()
        for e in range(N_EXPERTS):
            rows = win_ref[i * N_EXPERTS + e] + lax.broadcasted_iota(jnp.int32, (COMBINE_TM, COMBINE_WIN), 1)
            oh_ref[:, e * COMBINE_WIN:(e + 1) * COMBINE_WIN] = jnp.where(growt[:, e:e + 1] == rows, 1.0, 0.0).astype(BF16)
        acc_ref[...] = jnp.dot(oh_ref[...], wbuf_ref[slot], preferred_element_type=F32)

    @pl.when(fast_ref[i] == 0)
    def _():
        acc_ref[...] = jnp.zeros_like(acc_ref)
        for e in range(N_EXPERTS):
            lo = bounds_ref[(i * N_EXPERTS + e) * 2]
            hi = bounds_ref[(i * N_EXPERTS + e) * 2 + 1]
            for c in range(2):
                @pl.when((lo < (c + 1) * half) & (hi > c * half))
                def _():
                    cp = pltpu.make_async_copy(ye_hbm.at[e, pl.ds((i // tpg) * GROUP_ROWS + c * half, half), :],
                                               sbuf_ref, ssem)
                    cp.start()
                    cp.wait()
                    rows = lax.broadcasted_iota(jnp.int32, (COMBINE_TM, half), 1) + c * half
                    onehot = jnp.where(growt[:, e:e + 1] == rows, 1.0, 0.0).astype(BF16)
                    acc_ref[...] += jnp.dot(onehot, sbuf_ref[...], preferred_element_type=F32)

    o_ref[...] = x1_ref[...] + g2_ref[...] * _rms(acc_ref[...], gpost2_ref[...])


def _combine(growt, ye, x1, mod3, g_post2):
    T = x1.shape[0]
    tm = COMBINE_TM
    tpg = GROUP_TOKENS // tm
    g = growt.reshape(T // tm, tm, N_EXPERTS)
    lo = jnp.min(jnp.where(g >= 0, g, GROUP_ROWS), axis=1)
    hi = jnp.max(g, axis=1) + 1
    win = jnp.clip((lo // BF16_ROW_TILE) * BF16_ROW_TILE, 0, GROUP_ROWS - COMBINE_WIN)
    fast = jnp.all((hi <= lo) | (hi <= win + COMBINE_WIN), axis=1)
    bounds = jnp.stack([lo, hi], axis=-1)
    i32 = lambda a: a.reshape(-1).astype(jnp.int32)
    return pl.pallas_call(
        _combine_kernel,
        grid_spec=pltpu.PrefetchScalarGridSpec(
            num_scalar_prefetch=3,
            grid=(T // tm,),
            in_specs=[
                pl.BlockSpec((tm, N_EXPERTS), lambda i, *_: (i, 0)),
                pl.BlockSpec(memory_space=pl.ANY),
                pl.BlockSpec((tm, D_MODEL), lambda i, *_: (i, 0)),
                pl.BlockSpec((None, 1, D_MODEL), lambda i, *_: (i // tpg, 0, 5)),
                pl.BlockSpec((1, D_MODEL), lambda i, *_: (0, 0)),
            ],
            out_specs=pl.BlockSpec((tm, D_MODEL), lambda i, *_: (i, 0)),
            scratch_shapes=[
                pltpu.VMEM((2, N_EXPERTS * COMBINE_WIN, D_MODEL), BF16),
                pltpu.VMEM((tm, N_EXPERTS * COMBINE_WIN), BF16),
                pltpu.VMEM((tm, D_MODEL), F32),
                pltpu.VMEM((GROUP_ROWS // 2, D_MODEL), BF16),
                pltpu.SemaphoreType.DMA((2,)),
                pltpu.SemaphoreType.DMA(()),
            ],
        ),
        out_shape=jax.ShapeDtypeStruct((T, D_MODEL), F32),
        compiler_params=_cparams(("arbitrary",), 48),
        name="combine",
    )(i32(win), i32(fast), i32(bounds), growt, ye, x1, mod3, g_post2)


def _rope_tables(n):
    rows = n // GRID_W
    row = jnp.repeat(jnp.arange(rows), GRID_W).astype(F32)
    col = jnp.tile(jnp.arange(GRID_W), rows).astype(F32)
    inv = ROPE_BASE ** (-jnp.arange(0, HD // 2, 2, dtype=F32) / (HD // 2))
    ang = jnp.concatenate([row[:, None] * inv, col[:, None] * inv], axis=-1)
    cos, sin = jnp.cos(ang), jnp.sin(ang)
    return jnp.tile(cos, (1, 4)), jnp.tile(jnp.concatenate([-sin, sin], axis=-1), (1, 2))


def kernel(x_prompt, x_sample, cache_k, cache_v, state_rnn, c, c_ctx, w_mod, b_mod, g_pre1, g_post1, g_pre2, g_post2, w_in, conv_w, conv_b, lru_wr, lru_br, lru_wi, lru_bi, lru_lam, lam_q1, lam_k1, lam_q2, lam_k2, g_sub, w_a_out, w_b_out, w_o, w_router, w_e_gate, w_e_up, w_e_down):
    nbc, nc, _ = x_prompt.shape
    nbl, nl, _ = x_sample.shape
    assert nbc * nc == GROUP_TOKENS and nl == GROUP_TOKENS and w_mod.shape[0] == 1
    tc = nbc * nc
    x_c, x_l = x_prompt.reshape(tc, D_MODEL), x_sample.reshape(nbl * nl, D_MODEL)

    cvec = jnp.concatenate([c_ctx[None, :], c, jnp.zeros((SUBLANES - 1 - nbl, D_MODEL), F32)], axis=0)
    mod = _modulation(cvec, w_mod[0], b_mod)
    mod3 = mod.reshape(SUBLANES, 1, 6 * D_MODEL)

    proj = _in_proj(x_c, x_l, mod3, g_pre1, w_in[0].astype(BF16))

    rnn_w = (conv_w[0], conv_b, lru_wr[0], lru_wi[0], lru_br[0], lru_bi[0], lru_lam[0])
    ya_c, h_fin = _rnn_branch(proj, *rnn_w, jnp.zeros((nbc, 2, D_RNN), F32), seq=nc, nseq=SUBLANES,
                              nb=nbc // SUBLANES, row0=0)
    ya_l, _ = _rnn_branch(proj, *rnn_w, state_rnn[:, 0], seq=nl, nseq=1, nb=nbl, row0=tc // nl)

    lam_vecs = (lam_q1, lam_k1, lam_q2, lam_k2)
    on_c = _attention(proj, lam_vecs, g_sub, n=nc, nb=nbc, row0=0, tq=nc)
    past = cache_k.shape[2]
    on_l = _attention(proj, lam_vecs, g_sub, n=nl, nb=nbl, row0=tc // nl, tq=512,
                      cache=(cache_k[:, 0].reshape(nbl, past, N_HEADS * VD),
                             cache_v[:, 0].reshape(nbl, past, N_HEADS * VD)),
                      tables=_rope_tables(nl))

    x1, h2, aff_t = _post_mix(ya_c, ya_l, on_c, on_l, proj, x_c, x_l, mod3, g_post1, g_pre2, w_a_out[0].astype(BF16),
                              w_b_out[0].astype(BF16), w_o[0].astype(BF16), w_router[0].T)

    capc = EC_FACTOR * nc // N_EXPERTS
    idx_c, val_c, growt_c = _route(aff_t, n=nc, nsets=nbc, col0=0, row_stride=capc)
    idx_l, val_l, growt_l = _route(aff_t, n=nl, nsets=nbl, col0=tc // nl, row_stride=0)
    gid_c = idx_c[..., 0] + (jnp.arange(nbc, dtype=jnp.int32) * nc)[:, None, None]
    gid_l = idx_l[..., 0] + (tc + jnp.arange(nbl, dtype=jnp.int32) * nl)[:, None, None]
    gidx = jnp.concatenate([gid_c.transpose(1, 0, 2).reshape(N_EXPERTS, -1),
                            gid_l.transpose(1, 0, 2).reshape(N_EXPERTS, -1)], axis=1)
    vals = jnp.concatenate([val_c.transpose(1, 0, 2, 3).reshape(N_EXPERTS, -1, 1),
                            val_l.transpose(1, 0, 2, 3).reshape(N_EXPERTS, -1, 1)], axis=1)
    growt = jnp.concatenate([growt_c, growt_l], axis=0)

    ye = _experts(gidx, h2, vals, w_e_gate[0].astype(BF16), w_e_up[0].astype(BF16),
                  w_e_down[0].astype(BF16))
    x2 = _combine(growt, ye, x1, mod3, g_post2)

    y_prompt = x2[:tc].reshape(nbc, nc, D_MODEL)
    y_sample = x2[tc:].reshape(nbl, nl, D_MODEL)
    kblk = (2 * D_RNN + N_HEADS * VD) // (N_HEADS * VD)
    new_k = _take_cols(proj, tc, kblk, N_HEADS * VD).reshape(nbc, 1, nc, N_HEADS, 2 * HD)
    new_v = _take_cols(proj, tc, kblk + 1, N_HEADS * VD).reshape(nbc, 1, nc, N_HEADS, VD)
    return (y_prompt, y_sample, new_k, new_v, h_fin[:, None])
```

```python
import functools
import math

import jax
import jax.numpy as jnp
from jax import lax
from jax.experimental import pallas as pl
from jax.experimental.pallas import tpu as pltpu

F32 = jnp.float32
BF16 = jnp.bfloat16
HIGHEST = lax.Precision.HIGHEST

D_MODEL = 2048
D_RNN = 1024
RNN_BLOCKS = 8
RNN_BS = 128
CONV_W = 4
LRU_C = 8.0
N_HEADS = 8
HD = 64
VD = 128
N_EXPERTS = 16
D_EXPERT = 1024
EC_FACTOR = 2
EPS = 1e-6
IN_COLS = 9216
GRID_W = 64
ROPE_BASE = 10000.0
LAM_INIT = 0.8 - 0.6 * math.exp(-0.3 * 0)

GROUP_TOKENS = 4096
GROUP_ROWS = 512
SUBLANES = 8
LANES = 128
V7X_VMEM_BYTES = 64 * 1024 * 1024


def _cparams(semantics, vmem_mb):
    return pltpu.CompilerParams(dimension_semantics=semantics, vmem_limit_bytes=vmem_mb * 1024 * 1024)


def _sigmoid(x):
    return 0.5 * jnp.tanh(0.5 * x) + 0.5


def _rms(x, g):
    return x * lax.rsqrt(jnp.mean(x * x, axis=-1, keepdims=True) + EPS) * g


def _mod_kernel(c_ref, w_ref, b_ref, o_ref):
    c = c_ref[...]
    s = c * _sigmoid(c)
    o_ref[...] = jnp.dot(s, w_ref[...], precision=HIGHEST, preferred_element_type=F32) + b_ref[...]


def _modulation(cvec, w_mod, b_mod):
    tn = 1024
    return pl.pallas_call(
        _mod_kernel,
        grid=(6 * D_MODEL // tn,),
        in_specs=[
            pl.BlockSpec((SUBLANES, D_MODEL), lambda j: (0, 0)),
            pl.BlockSpec((D_MODEL, tn), lambda j: (0, j)),
            pl.BlockSpec((1, tn), lambda j: (0, j)),
        ],
        out_specs=pl.BlockSpec((SUBLANES, tn), lambda j: (0, j)),
        out_shape=jax.ShapeDtypeStruct((SUBLANES, 6 * D_MODEL), F32),
        compiler_params=_cparams(("arbitrary",), 40),
        name="modulation",
    )(cvec, w_mod, b_mod)


def _in_kernel(xc_ref, xl_ref, sh_ref, sc_ref, g_ref, w_ref, o_ref, h_ref, *, ctx_tiles):
    i, j = pl.program_id(0), pl.program_id(1)
    rows = 128

    def prologue(x_ref):
        def body(c, carry):
            r0 = pl.multiple_of(c * rows, rows)
            h = _rms(x_ref[pl.ds(r0, rows), :], g_ref[...]) * (1.0 + sc_ref[...]) + sh_ref[...]
            h_ref[pl.ds(r0, rows), :] = h.astype(BF16)
            return carry

        lax.fori_loop(0, h_ref.shape[0] // rows, body, 0)

    @pl.when((j == 0) & (i < ctx_tiles))
    def _():
        prologue(xc_ref)

    @pl.when((j == 0) & (i >= ctx_tiles))
    def _():
        prologue(xl_ref)

    o_ref[...] = jnp.dot(h_ref[...], w_ref[...], preferred_element_type=F32)


def _in_proj(x_c, x_l, mod3, g_pre1, w_in_bf):
    T = x_c.shape[0] + x_l.shape[0]
    tm, tn = 1024, 1024
    tpg = GROUP_TOKENS // tm
    nct = x_c.shape[0] // tm
    return pl.pallas_call(
        functools.partial(_in_kernel, ctx_tiles=nct),
        grid=(T // tm, IN_COLS // tn),
        in_specs=[
            pl.BlockSpec((tm, D_MODEL), lambda i, j: (jnp.minimum(i, nct - 1), 0), pipeline_mode=pl.Buffered(1)),
            pl.BlockSpec((tm, D_MODEL), lambda i, j: (jnp.maximum(i - nct, 0), 0)),
            pl.BlockSpec((None, 1, D_MODEL), lambda i, j: (i // tpg, 0, 0)),
            pl.BlockSpec((None, 1, D_MODEL), lambda i, j: (i // tpg, 0, 1)),
            pl.BlockSpec((1, D_MODEL), lambda i, j: (0, 0)),
            pl.BlockSpec((D_MODEL, tn), lambda i, j: (0, j)),
        ],
        out_specs=pl.BlockSpec((tm, tn), lambda i, j: (i, j)),
        out_shape=jax.ShapeDtypeStruct((T, IN_COLS), F32),
        scratch_shapes=[pltpu.VMEM((tm, D_MODEL), BF16)],
        compiler_params=_cparams(("arbitrary", "arbitrary"), 56),
        name="in_proj",
    )(x_c, x_l, mod3, mod3, g_pre1, w_in_bf)


def _copy_kernel(x_ref, o_ref):
    o_ref[...] = x_ref[...]


def _take_cols(proj, rows, col_block, width):
    tm = 1024
    return pl.pallas_call(
        _copy_kernel,
        grid=(rows // tm,),
        in_specs=[pl.BlockSpec((tm, width), lambda i: (i, col_block))],
        out_specs=pl.BlockSpec((tm, width), lambda i: (i, 0)),
        out_shape=jax.ShapeDtypeStruct((rows, width), proj.dtype),
        compiler_params=_cparams(("arbitrary",), 40),
        name="take_cols",
    )(proj)


def _gelu_tanh(x):
    return 0.5 * x * (1.0 + jnp.tanh(math.sqrt(2.0 / math.pi) * (x + 0.044715 * (x * x * x))))


def _softplus(x):
    return jnp.maximum(x, 0.0) + jnp.log(1.0 + jnp.exp(-jnp.abs(x)))


def _split_bf16(x):
    hi = x.astype(BF16)
    return hi, (x - hi.astype(F32)).astype(BF16)


def _dot_3pass(x, w_hi, w_lo):
    x_hi, x_lo = _split_bf16(x)
    d = functools.partial(jnp.dot, preferred_element_type=F32)
    return d(x_hi, w_hi) + (d(x_lo, w_hi) + d(x_hi, w_lo))


def _rnn_kernel(xr_ref, gr_ref, cw_ref, cb_ref, wr_ref, wi_ref, br_ref, bi_ref, lam_ref, h0_ref,
                ya_ref, hf_ref, xp_ref, a_ref, b_ref, h_ref, p_ref, *, seq, nseq):
    n = seq * nseq
    seg = n // SUBLANES
    pitch = seg + SUBLANES
    zeros8 = jnp.zeros((SUBLANES, LANES), F32)
    xp_ref[0:SUBLANES, :] = zeros8
    xp_ref[SUBLANES + n:2 * SUBLANES + n, :] = zeros8
    xp_ref[SUBLANES:SUBLANES + n, :] = xr_ref[...]
    cw = cw_ref[...]
    cb = cb_ref[...]
    sp = _softplus(-lam_ref[...])
    w_r = [_split_bf16(wr_ref[d]) for d in range(2)]
    w_i = [_split_bf16(wi_ref[d]) for d in range(2)]

    def gates(s, carry):
        r0 = pl.multiple_of(s * seg, SUBLANES)
        taps = [xp_ref[pl.ds(r0 + 6 + k, seg), :] for k in range(CONV_W)]
        if nseq > 1:
            t = (r0 + lax.broadcasted_iota(jnp.int32, (seg, LANES), 0)) % seq
            taps[0] = jnp.where(t >= 2, taps[0], 0.0)
            taps[1] = jnp.where(t >= 1, taps[1], 0.0)
            taps[3] = jnp.where(t <= seq - 2, taps[3], 0.0)
        xc = cb + taps[0] * cw[0:1]
        xc = xc + taps[1] * cw[1:2]
        xc = xc + taps[2] * cw[2:3]
        xc = xc + taps[3] * cw[3:4]
        p0 = pl.multiple_of(s * pitch, SUBLANES)
        for d in range(2):
            r = _sigmoid(_dot_3pass(xc, *w_r[d]) + br_ref[d:d + 1, :])
            i = _sigmoid(_dot_3pass(xc, *w_i[d]) + bi_ref[d:d + 1, :])
            a = jnp.exp((-LRU_C) * r * sp[d:d + 1, :])
            y = 1.0 - a * a
            a_ref[d, pl.ds(p0, seg), :] = a
            b_ref[d, pl.ds(p0, seg), :] = jnp.where(y > 0.0, y * lax.rsqrt(y), 0.0) * (i * xc)
        return carry

    lax.fori_loop(0, SUBLANES, gates, 0)

    def scan(j, carry):
        hf, af, hb, ab = carry
        a0 = a_ref[0, pl.ds(j, SUBLANES, stride=pitch), :]
        b0 = b_ref[0, pl.ds(j, SUBLANES, stride=pitch), :]
        hf = a0 * hf + b0
        af = a0 * af
        h_ref[0, pl.ds(j, SUBLANES, stride=pitch), :] = hf
        p_ref[0, pl.ds(j, SUBLANES, stride=pitch), :] = af
        jb = seg - 1 - j
        a1 = a_ref[1, pl.ds(jb, SUBLANES, stride=pitch), :]
        b1 = b_ref[1, pl.ds(jb, SUBLANES, stride=pitch), :]
        hb = a1 * hb + b1
        ab = a1 * ab
        h_ref[1, pl.ds(jb, SUBLANES, stride=pitch), :] = hb
        p_ref[1, pl.ds(jb, SUBLANES, stride=pitch), :] = ab
        return hf, af, hb, ab

    ones8 = jnp.ones((SUBLANES, LANES), F32)
    ef, pf, eb, pb = lax.fori_loop(0, seg, scan, (zeros8, ones8, zeros8, ones8), unroll=8)

    spq = SUBLANES // nseq
    cf = [None] * SUBLANES
    cbk = [None] * SUBLANES
    for q in range(nseq):
        h0 = h0_ref[q]
        first, last = q * spq, q * spq + spq - 1
        cf[first] = h0[0:1, :]
        for s in range(first + 1, last + 1):
            cf[s] = ef[s - 1:s, :] + pf[s - 1:s, :] * cf[s - 1]
        cbk[last] = h0[1:2, :]
        for s in range(last - 1, first - 1, -1):
            cbk[s] = eb[s + 1:s + 2, :] + pb[s + 1:s + 2, :] * cbk[s + 1]
        hf_ref[q, 0:1, :] = ef[last:last + 1, :] + pf[last:last + 1, :] * cf[last]
        hf_ref[q, 1:2, :] = eb[first:first + 1, :] + pb[first:first + 1, :] * cbk[first]

    for s in range(SUBLANES):
        hfw = h_ref[0, s * pitch:s * pitch + seg, :] + p_ref[0, s * pitch:s * pitch + seg, :] * cf[s]
        hbw = h_ref[1, s * pitch:s * pitch + seg, :] + p_ref[1, s * pitch:s * pitch + seg, :] * cbk[s]
        y = (hfw + hbw) * _gelu_tanh(gr_ref[s * seg:(s + 1) * seg, :])
        ya_ref[s * seg:(s + 1) * seg, :] = y.astype(ya_ref.dtype)


def _rnn_branch(proj, conv_w, conv_b, wr, wi, br, bi, lam, h0, *, seq, nseq, nb, row0):
    n = seq * nseq
    seg = n // SUBLANES
    pitch = seg + SUBLANES
    return pl.pallas_call(
        functools.partial(_rnn_kernel, seq=seq, nseq=nseq),
        grid=(nb, RNN_BLOCKS),
        in_specs=[
            pl.BlockSpec((n, RNN_BS), lambda b, k: (row0 + b, k)),
            pl.BlockSpec((n, RNN_BS), lambda b, k: (row0 + b, RNN_BLOCKS + k)),
            pl.BlockSpec((CONV_W, RNN_BS), lambda b, k: (0, k)),
            pl.BlockSpec((1, RNN_BS), lambda b, k: (0, k)),
            pl.BlockSpec((2, None, RNN_BS, RNN_BS), lambda b, k: (0, k, 0, 0)),
            pl.BlockSpec((2, None, RNN_BS, RNN_BS), lambda b, k: (0, k, 0, 0)),
            pl.BlockSpec((2, RNN_BS), lambda b, k: (0, k)),
            pl.BlockSpec((2, RNN_BS), lambda b, k: (0, k)),
            pl.BlockSpec((2, RNN_BS), lambda b, k: (0, k)),
            pl.BlockSpec((nseq, 2, RNN_BS), lambda b, k: (b, 0, k)),
        ],
        out_specs=[
            pl.BlockSpec((n, RNN_BS), lambda b, k: (b, k)),
            pl.BlockSpec((nseq, 2, RNN_BS), lambda b, k: (b, 0, k)),
        ],
        out_shape=[jax.ShapeDtypeStruct((nb * n, D_RNN), BF16), jax.ShapeDtypeStruct((nb * nseq, 2, D_RNN), F32)],
        scratch_shapes=[
            pltpu.VMEM((n + 2 * SUBLANES, RNN_BS), F32),
        ] + [pltpu.VMEM((2, SUBLANES * pitch, RNN_BS), F32)] * 4,
        compiler_params=_cparams(("arbitrary", "arbitrary"), 40),
        name=f"rnn_n{n}",
    )(proj, proj, conv_w, conv_b, wr, wi, br, bi, lam, h0)


def _rope(x, cos, sin_signed):
    lane = lax.broadcasted_iota(jnp.int32, x.shape, 1)
    partner = jnp.where((lane % HD) < HD // 2, pltpu.roll(x, LANES - HD // 2, 1), pltpu.roll(x, HD // 2, 1))
    return x * cos + partner * sin_signed


ATTN_SUB = 128


def _attn_kernel(*refs, n, npast, tq, rope):
    if rope:
        (lq1, lk1, lq2, lk2, q_ref, k_ref, v_ref, gsub_ref, ck_ref, cv_ref, cosk_ref, sink_ref, cosq_ref, sinq_ref,
         o_ref, kb_ref, vb_ref, s_ref, p_ref) = refs
    else:
        lq1, lk1, lq2, lk2, q_ref, k_ref, v_ref, gsub_ref, o_ref, kb_ref, vb_ref, s_ref, p_ref = refs
    chunk = 256
    nk = n + npast

    @pl.when(pl.program_id(2) == 0)
    def _():
        ones_col = jnp.where(lax.broadcasted_iota(jnp.int32, (chunk, VD), 1) == 0, 1.0, 0.0).astype(BF16)
        if npast:
            for r0 in range(0, npast, chunk):
                kb_ref[:, r0:r0 + chunk] = ck_ref[r0:r0 + chunk, :].T.astype(BF16)
                vb_ref[r0:r0 + chunk, VD:2 * VD] = ones_col
            vb_ref[0:npast, 0:VD] = cv_ref[...].astype(BF16)

        for r0 in range(0, n, chunk):
            k = k_ref[r0:r0 + chunk, :]
            if rope:
                k = _rope(k, cosk_ref[r0:r0 + chunk, :], sink_ref[r0:r0 + chunk, :])
            kb_ref[:, npast + r0:npast + r0 + chunk] = k.T.astype(BF16)
            vb_ref[npast + r0:npast + r0 + chunk, 0:VD] = v_ref[r0:r0 + chunk, :].astype(BF16)
            vb_ref[npast + r0:npast + r0 + chunk, VD:2 * VD] = ones_col

    lam = (jnp.exp(jnp.sum(lq1[...] * lk1[...], axis=-1, keepdims=True))
           - jnp.exp(jnp.sum(lq2[...] * lk2[...], axis=-1, keepdims=True)) + LAM_INIT)
    q = q_ref[...]
    if rope:
        q = _rope(q, cosq_ref[...], sinq_ref[...])
    q = q * (HD ** -0.5 * math.log2(math.e))
    lane = lax.broadcasted_iota(jnp.int32, (ATTN_SUB, VD), 1)
    nt = (((1,), (1,)), ((), ()))
    kc = 2 * LANES
    key_chunks = [(c0, min(kc, nk - c0)) for c0 in range(0, nk, kc)]
    for sub in range(tq // ATTN_SUB):
        qs = q[sub * ATTN_SUB:(sub + 1) * ATTN_SUB, :]
        outs = []
        for comp in range(2):
            unit = 2 * sub + comp
            qc = jnp.where((lane < HD) == (comp == 0), qs, 0.0).astype(BF16)
            m_run = None
            for c0, w in key_chunks:
                s = jnp.dot(qc, kb_ref[:, c0:c0 + w], preferred_element_type=F32)
                s_ref[unit, :, c0:c0 + w] = s
                for l0 in range(0, w, LANES):
                    piece = s[:, l0:l0 + LANES]
                    m_run = piece if m_run is None else jnp.maximum(m_run, piece)
            m = jnp.max(m_run, axis=-1, keepdims=True)
            for c0, w in key_chunks:
                p_ref[unit, :, c0:c0 + w] = jnp.exp2(s_ref[unit, :, c0:c0 + w] - m).astype(BF16)
            outs.append(jnp.dot(p_ref[unit], vb_ref[...], preferred_element_type=F32))
        o = (outs[0][:, 0:VD] * (1.0 / outs[0][:, VD:VD + 1])
             - outs[1][:, 0:VD] * (lam / outs[1][:, VD:VD + 1]))
        o = _rms(o, gsub_ref[...]) * (1.0 - LAM_INIT)
        o_ref[sub * ATTN_SUB:(sub + 1) * ATTN_SUB, :] = o.astype(o_ref.dtype)


def _attention(proj, lam_vecs, g_sub, *, n, nb, row0, tq, cache=None, tables=None):
    rope = tables is not None
    npast = 0 if cache is None else cache[0].shape[1]
    nk = n + npast
    qpr = n // tq
    qcol, kcol, vcol = 2 * RNN_BLOCKS, 2 * RNN_BLOCKS + N_HEADS, 2 * RNN_BLOCKS + 2 * N_HEADS
    vec = pl.BlockSpec((1, HD), lambda b, h, t: (0, 0))
    in_specs = [vec, vec, vec, vec,
                pl.BlockSpec((tq, VD), lambda b, h, t: ((row0 + b) * qpr + t, qcol + h)),
                pl.BlockSpec((n, VD), lambda b, h, t: (row0 + b, kcol + h)),
                pl.BlockSpec((n, VD), lambda b, h, t: (row0 + b, vcol + h)),
                pl.BlockSpec((1, VD), lambda b, h, t: (0, 0))]
    args = [*lam_vecs, proj, proj, proj, g_sub]
    if rope:
        ck, cv = cache
        cos_t, sin_t = tables
        in_specs += [pl.BlockSpec((None, npast, VD), lambda b, h, t: (b, 0, h)),
                     pl.BlockSpec((None, npast, VD), lambda b, h, t: (b, 0, h)),
                     pl.BlockSpec((n, VD), lambda b, h, t: (0, 0), pipeline_mode=pl.Buffered(1)),
                     pl.BlockSpec((n, VD), lambda b, h, t: (0, 0), pipeline_mode=pl.Buffered(1)),
                     pl.BlockSpec((tq, VD), lambda b, h, t: (t, 0)),
                     pl.BlockSpec((tq, VD), lambda b, h, t: (t, 0))]
        args += [ck, cv, cos_t, sin_t, cos_t, sin_t]
    return pl.pallas_call(
        functools.partial(_attn_kernel, n=n, npast=npast, tq=tq, rope=rope),
        grid=(nb, N_HEADS, qpr),
        in_specs=in_specs,
        out_specs=pl.BlockSpec((tq, VD), lambda b, h, t: (b * qpr + t, h)),
        out_shape=jax.ShapeDtypeStruct((nb * n, N_HEADS * VD), BF16),
        scratch_shapes=[pltpu.VMEM((VD, nk), BF16), pltpu.VMEM((nk, 2 * VD), BF16),
                        pltpu.VMEM((2 * tq // ATTN_SUB, ATTN_SUB, nk), F32),
                        pltpu.VMEM((2 * tq // ATTN_SUB, ATTN_SUB, nk), BF16)],
        compiler_params=_cparams(("arbitrary", "arbitrary", "arbitrary"), 56),
        name=f"attn_n{n}",
    )(*args)


POST_TM = 512
POST_UNIT = 256


def _merge_kernel(yac_ref, yal_ref, onc_ref, onl_ref, ga0_ref, ga1_ref, gb0_ref, gb1_ref, wa_ref, wb_ref, m_ref,
                  *, ctx_tiles):
    is_ctx = pl.program_id(0) < ctx_tiles
    half = D_MODEL // 2
    for r0 in range(0, POST_TM, POST_UNIT):
        rs = slice(r0, r0 + POST_UNIT)
        ya = jnp.where(is_ctx, yac_ref[rs, :], yal_ref[rs, :])
        on = jnp.where(is_ctx, onc_ref[rs, :], onl_ref[rs, :])
        oa = jnp.dot(ya, wa_ref[...], preferred_element_type=F32)
        ob = jnp.dot(on, wb_ref[...], preferred_element_type=F32)
        m_ref[rs, 0:half] = (_sigmoid(ga0_ref[rs, :]) * oa[:, 0:half]
                             + _sigmoid(gb0_ref[rs, :]) * ob[:, 0:half]).astype(BF16)
        m_ref[rs, half:] = (_sigmoid(ga1_ref[rs, :]) * oa[:, half:]
                            + _sigmoid(gb1_ref[rs, :]) * ob[:, half:]).astype(BF16)


def _outproj_kernel(m_ref, xc_ref, xl_ref, g1_ref, sh2_ref, sc2_ref, gpost1_ref, gpre2_ref, wo_ref, wrt_ref,
                    x1_ref, h2_ref, aff_ref, *, ctx_tiles):
    is_ctx = pl.program_id(0) < ctx_tiles
    nt = functools.partial(lax.dot_general, dimension_numbers=(((1,), (1,)), ((), ())), preferred_element_type=F32)
    w_hi, w_lo = _split_bf16(wrt_ref[...])
    units = [slice(r0, r0 + POST_UNIT) for r0 in range(0, POST_TM, POST_UNIT)]
    ts = [jnp.dot(m_ref[rs, :], wo_ref[...], preferred_element_type=F32) for rs in units]
    for rs, t in zip(units, ts):
        x1 = jnp.where(is_ctx, xc_ref[rs, :], xl_ref[rs, :]) + g1_ref[...] * _rms(t, gpost1_ref[...])
        x1_ref[rs, :] = x1
        h2 = _rms(x1, gpre2_ref[...]) * (1.0 + sc2_ref[...]) + sh2_ref[...]
        h2_ref[rs, :] = h2
        h_hi, h_lo = _split_bf16(h2)
        logits = nt(w_hi, h_hi) + (nt(w_lo, h_hi) + nt(w_hi, h_lo))
        e = jnp.exp(logits - jnp.max(logits, axis=0, keepdims=True))
        aff_ref[:, rs] = e / jnp.sum(e, axis=0, keepdims=True)


def _post_mix(ya_c, ya_l, on_c, on_l, proj, x_c, x_l, mod3, g_post1, g_pre2, wa, wb, wo, wrt):
    T = x_c.shape[0] + x_l.shape[0]
    tm = POST_TM
    tpg = GROUP_TOKENS // tm
    nct = ya_c.shape[0] // tm
    ctx_blk = lambda w: pl.BlockSpec((tm, w), lambda i: (jnp.minimum(i, nct - 1), 0), pipeline_mode=pl.Buffered(1))
    lat_blk = lambda w: pl.BlockSpec((tm, w), lambda i: (jnp.maximum(i - nct, 0), 0))
    half = D_MODEL // 2
    gate0 = (2 * D_RNN + 2 * N_HEADS * VD + N_HEADS * VD) // half
    row = lambda c: pl.BlockSpec((None, 1, D_MODEL), lambda i: (i // tpg, 0, c))
    const = lambda shape: pl.BlockSpec(shape, lambda i: (0,) * len(shape), pipeline_mode=pl.Buffered(1))
    merged = pl.pallas_call(
        functools.partial(_merge_kernel, ctx_tiles=nct),
        grid=(T // tm,),
        in_specs=[
            ctx_blk(D_RNN), lat_blk(D_RNN), ctx_blk(N_HEADS * VD), lat_blk(N_HEADS * VD),
            pl.BlockSpec((tm, half), lambda i: (i, gate0)),
            pl.BlockSpec((tm, half), lambda i: (i, gate0 + 1)),
            pl.BlockSpec((tm, half), lambda i: (i, gate0 + 2)),
            pl.BlockSpec((tm, half), lambda i: (i, gate0 + 3)),
            const((D_RNN, D_MODEL)), const((N_HEADS * VD, D_MODEL)),
        ],
        out_specs=pl.BlockSpec((tm, D_MODEL), lambda i: (i, 0)),
        out_shape=jax.ShapeDtypeStruct((T, D_MODEL), BF16),
        compiler_params=_cparams(("arbitrary",), 56),
        name="merge",
    )(ya_c, ya_l, on_c, on_l, proj, proj, proj, proj, wa, wb)
    return pl.pallas_call(
        functools.partial(_outproj_kernel, ctx_tiles=nct),
        grid=(T // tm,),
        in_specs=[
            pl.BlockSpec((tm, D_MODEL), lambda i: (i, 0)),
            ctx_blk(D_MODEL), lat_blk(D_MODEL),
            row(2), row(3), row(4),
            const((1, D_MODEL)), const((1, D_MODEL)),
            const((D_MODEL, D_MODEL)),
            const((N_EXPERTS, D_MODEL)),
        ],
        out_specs=[
            pl.BlockSpec((tm, D_MODEL), lambda i: (i, 0)),
            pl.BlockSpec((tm, D_MODEL), lambda i: (i, 0)),
            pl.BlockSpec((N_EXPERTS, tm), lambda i: (0, i)),
        ],
        out_shape=[jax.ShapeDtypeStruct((T, D_MODEL), F32), jax.ShapeDtypeStruct((T, D_MODEL), F32),
                   jax.ShapeDtypeStruct((N_EXPERTS, T), F32)],
        compiler_params=_cparams(("arbitrary",), 56),
        name="out_proj",
    )(merged, x_c, x_l, mod3, mod3, mod3, g_post1, g_pre2, wo, wrt)


def _cumsum_excl(x, tri):
    n = x.shape[1]
    blk = tri.shape[0]
    outs = []
    carry = jnp.zeros((x.shape[0], 1), F32)
    for c in range(n // blk):
        xc = x[:, c * blk:(c + 1) * blk]
        outs.append(jnp.dot(xc.astype(BF16), tri, preferred_element_type=F32) + carry)
        carry = carry + jnp.sum(xc, axis=1, keepdims=True)
    return outs[0] if len(outs) == 1 else jnp.concatenate(outs, axis=1)


def _route_kernel(aff_ref, idx_ref, val_ref, growt_ref, *, n, cap, row_stride):
    aff = aff_ref[...]

    def bisect(_, lohi):
        lo, hi = lohi
        mid = lo + ((hi - lo) >> 1)
        cnt = jnp.sum((aff >= pltpu.bitcast(mid, F32)).astype(jnp.int32), axis=1, keepdims=True)
        ok = cnt >= cap
        return jnp.where(ok, mid, lo), jnp.where(ok, hi, mid)

    lo0 = jnp.zeros((N_EXPERTS, 1), jnp.int32)
    hi0 = jnp.full((N_EXPERTS, 1), 0x7F800000, jnp.int32)
    thr, _ = lax.fori_loop(0, 31, bisect, (lo0, hi0))
    gt = aff >= pltpu.bitcast(thr + 1, F32)
    eq = (aff >= pltpu.bitcast(thr, F32)) & jnp.logical_not(gt)
    need = cap - jnp.sum(gt.astype(jnp.int32), axis=1, keepdims=True)
    blk = min(n, 256)
    tri = (lax.broadcasted_iota(jnp.int32, (blk, blk), 0) < lax.broadcasted_iota(jnp.int32, (blk, blk), 1)).astype(BF16)
    eqpos = _cumsum_excl(eq.astype(F32), tri)
    sel = gt | (eq & (eqpos < need.astype(F32)))
    pos = _cumsum_excl(sel.astype(F32), tri).astype(jnp.int32)
    slot = jnp.where(sel, pos, -1)
    grow = jnp.where(sel, pos + pl.program_id(0) * row_stride, -1)
    growt_ref[...] = grow.astype(F32).T.astype(jnp.int32)

    rc = min(cap, 32)
    tok = lax.broadcasted_iota(jnp.int32, (rc, n), 1)
    for e in range(N_EXPERTS):
        slot_e = slot[e:e + 1, :]
        aff_e = aff[e:e + 1, :]

        def extract(c, carry):
            r0 = pl.multiple_of(c * rc, rc)
            hit = slot_e == (lax.broadcasted_iota(jnp.int32, (rc, n), 0) + r0)
            idx_ref[e, pl.ds(r0, rc), :] = jnp.sum(jnp.where(hit, tok, 0), axis=1, keepdims=True)
            val_ref[e, pl.ds(r0, rc), :] = jnp.sum(jnp.where(hit, aff_e, 0.0), axis=1, keepdims=True)
            return carry

        lax.fori_loop(0, cap // rc, extract, 0)


def _route(aff_t, *, n, nsets, col0, row_stride):
    cap = EC_FACTOR * n // N_EXPERTS
    kern = functools.partial(_route_kernel, n=n, cap=cap, row_stride=row_stride)
    return pl.pallas_call(
        kern,
        grid=(nsets,),
        in_specs=[pl.BlockSpec((N_EXPERTS, n), lambda s: (0, col0 + s))],
        out_specs=[
            pl.BlockSpec((None, N_EXPERTS, cap, 1), lambda s: (s, 0, 0, 0)),
            pl.BlockSpec((None, N_EXPERTS, cap, 1), lambda s: (s, 0, 0, 0)),
            pl.BlockSpec((n, N_EXPERTS), lambda s: (s, 0)),
        ],
        out_shape=[jax.ShapeDtypeStruct((nsets, N_EXPERTS, cap, 1), jnp.int32),
                   jax.ShapeDtypeStruct((nsets, N_EXPERTS, cap, 1), F32),
                   jax.ShapeDtypeStruct((nsets * n, N_EXPERTS), jnp.int32)],
        compiler_params=_cparams(("arbitrary",), 48),
        name=f"route_n{n}",
    )(aff_t)


def _expert_kernel(cur_ref, nxt_ref, h2_hbm, val_ref, wg_ref, wu_ref, wd_ref, ye_ref, xs0_ref, xs1_ref, sem, *, rows):
    step = pl.program_id(0) * pl.num_programs(1) + pl.program_id(1)
    nsteps = pl.num_programs(0) * pl.num_programs(1)
    bufs = (xs0_ref, xs1_ref)

    def row_copy(idx_ref, r, s):
        return pltpu.make_async_copy(h2_hbm.at[pl.ds(idx_ref[0, r], 1), :], bufs[s].at[pl.ds(r, 1), :], sem.at[s])

    def slot_wait(s):
        pltpu.make_async_copy(h2_hbm.at[pl.ds(0, rows), :], bufs[s], sem.at[s]).wait()

    @pl.when(step == 0)
    def _():
        def issue(r, carry):
            row_copy(cur_ref, r, 0).start()
            return carry

        lax.fori_loop(0, rows, issue, 0, unroll=8)

    def body(s):
        for r in range(rows):
            row_copy(nxt_ref, r, 1 - s).start()
        slot_wait(s)
        half = rows // 2
        for c in range(2):
            xs = bufs[s][c * half:(c + 1) * half, :].astype(BF16)
            g = jnp.dot(xs, wg_ref[...], preferred_element_type=F32)
            u = jnp.dot(xs, wu_ref[...], preferred_element_type=F32)
            hid = (g * _sigmoid(g)) * u
            y = jnp.dot(hid.astype(BF16), wd_ref[...], preferred_element_type=F32) * val_ref[c * half:(c + 1) * half, :]
            ye_ref[c * half:(c + 1) * half, :] = y.astype(ye_ref.dtype)

        @pl.when(step == nsteps - 1)
        def _():
            slot_wait(1 - s)

    for s in range(2):
        pl.when(step % 2 == s)(functools.partial(body, s))


def _experts(gidx, h2, vals, wg, wu, wd):
    rows_per_expert = vals.shape[1]
    rows = GROUP_ROWS
    groups = rows_per_expert // rows
    last = N_EXPERTS * groups - 1
    return pl.pallas_call(
        functools.partial(_expert_kernel, rows=rows),
        grid=(N_EXPERTS, groups),
        in_specs=[
            pl.BlockSpec((None, 1, rows), lambda e, g: (e * groups + g, 0, 0), memory_space=pltpu.SMEM),
            pl.BlockSpec((None, 1, rows), lambda e, g: (jnp.minimum(e * groups + g + 1, last), 0, 0),
                         memory_space=pltpu.SMEM),
            pl.BlockSpec(memory_space=pl.ANY),
            pl.BlockSpec((None, rows, 1), lambda e, g: (e, g, 0)),
            pl.BlockSpec((None, D_MODEL, D_EXPERT), lambda e, g: (e, 0, 0)),
            pl.BlockSpec((None, D_MODEL, D_EXPERT), lambda e, g: (e, 0, 0)),
            pl.BlockSpec((None, D_EXPERT, D_MODEL), lambda e, g: (e, 0, 0)),
        ],
        out_specs=pl.BlockSpec((None, rows, D_MODEL), lambda e, g: (e, g, 0)),
        out_shape=jax.ShapeDtypeStruct((N_EXPERTS, rows_per_expert, D_MODEL), BF16),
        scratch_shapes=[pltpu.VMEM((rows, D_MODEL), F32), pltpu.VMEM((rows, D_MODEL), F32),
                        pltpu.SemaphoreType.DMA((2,))],
        compiler_params=_cparams(("arbitrary", "arbitrary"), 56),
        name="experts",
    )(gidx.reshape(N_EXPERTS * groups, 1, rows), gidx.reshape(N_EXPERTS * groups, 1, rows), h2, vals, wg, wu, wd)


COMBINE_TM = 256
COMBINE_WIN = 128
BF16_ROW_TILE = 16


def _combine_kernel(win_ref, fast_ref, bounds_ref, growt_ref, ye_hbm, x1_ref, g2_ref, gpost2_ref, o_ref,
                    wbuf_ref, oh_ref, acc_ref, sbuf_ref, wsem, ssem):
    i = pl.program_id(0)
    nsteps = pl.num_programs(0)
    tpg = GROUP_TOKENS // COMBINE_TM
    slot = i % 2
    half = GROUP_ROWS // 2
    growt = growt_ref[...]

    def window_copy(step, e, s):
        start = pl.multiple_of((step // tpg) * GROUP_ROWS + win_ref[step * N_EXPERTS + e], BF16_ROW_TILE)
        return pltpu.make_async_copy(ye_hbm.at[e, pl.ds(start, COMBINE_WIN), :],
                                     wbuf_ref.at[s, pl.ds(e * COMBINE_WIN, COMBINE_WIN), :], wsem.at[s])

    def fetch(step, s):
        for e in range(N_EXPERTS):
            window_copy(step, e, s).start()

    @pl.when((i == 0) & (fast_ref[0] == 1))
    def _():
        fetch(0, 0)

    nxt = jnp.minimum(i + 1, nsteps - 1)

    @pl.when((i + 1 < nsteps) & (fast_ref[nxt] == 1))
    def _():
        fetch(nxt, 1 - slot)

    @pl.when(fast_ref[i] == 1)
    def _():
        for e in range(N_EXPERTS):
            window_copy(i, e, slot).wait()
        for e in range(N_EXPERTS):
            rows = win_ref[i * N_EXPERTS + e] + lax.broadcasted_iota(jnp.int32, (COMBINE_TM, COMBINE_WIN), 1)
            oh_ref[:, e * COMBINE_WIN:(e + 1) * COMBINE_WIN] = jnp.where(growt[:, e:e + 1] == rows, 1.0, 0.0).astype(BF16)
        acc_ref[...] = jnp.dot(oh_ref[...], wbuf_ref[slot], preferred_element_type=F32)

    @pl.when(fast_ref[i] == 0)
    def _():
        acc_ref[...] = jnp.zeros_like(acc_ref)
        for e in range(N_EXPERTS):
            lo = bounds_ref[(i * N_EXPERTS + e) * 2]
            hi = bounds_ref[(i * N_EXPERTS + e) * 2 + 1]
            for c in range(2):
                @pl.when((lo < (c + 1) * half) & (hi > c * half))
                def _():
                    cp = pltpu.make_async_copy(ye_hbm.at[e, pl.ds((i // tpg) * GROUP_ROWS + c * half, half), :],
                                               sbuf_ref, ssem)
                    cp.start()
                    cp.wait()
                    rows = lax.broadcasted_iota(jnp.int32, (COMBINE_TM, half), 1) + c * half
                    onehot = jnp.where(growt[:, e:e + 1] == rows, 1.0, 0.0).astype(BF16)
                    acc_ref[...] += jnp.dot(onehot, sbuf_ref[...], preferred_element_type=F32)

    o_ref[...] = x1_ref[...] + g2_ref[...] * _rms(acc_ref[...], gpost2_ref[...])


def _combine(growt, ye, x1, mod3, g_post2):
    T = x1.shape[0]
    tm = COMBINE_TM
    tpg = GROUP_TOKENS // tm
    g = growt.reshape(T // tm, tm, N_EXPERTS)
    lo = jnp.min(jnp.where(g >= 0, g, GROUP_ROWS), axis=1)
    hi = jnp.max(g, axis=1) + 1
    win = jnp.clip((lo // BF16_ROW_TILE) * BF16_ROW_TILE, 0, GROUP_ROWS - COMBINE_WIN)
    fast = jnp.all((hi <= lo) | (hi <= win + COMBINE_WIN), axis=1)
    bounds = jnp.stack([lo, hi], axis=-1)
    i32 = lambda a: a.reshape(-1).astype(jnp.int32)
    return pl.pallas_call(
        _combine_kernel,
        grid_spec=pltpu.PrefetchScalarGridSpec(
            num_scalar_prefetch=3,
            grid=(T // tm,),
            in_specs=[
                pl.BlockSpec((tm, N_EXPERTS), lambda i, *_: (i, 0)),
                pl.BlockSpec(memory_space=pl.ANY),
                pl.BlockSpec((tm, D_MODEL), lambda i, *_: (i, 0)),
                pl.BlockSpec((None, 1, D_MODEL), lambda i, *_: (i // tpg, 0, 5)),
                pl.BlockSpec((1, D_MODEL), lambda i, *_: (0, 0)),
            ],
            out_specs=pl.BlockSpec((tm, D_MODEL), lambda i, *_: (i, 0)),
            scratch_shapes=[
                pltpu.VMEM((2, N_EXPERTS * COMBINE_WIN, D_MODEL), BF16),
                pltpu.VMEM((tm, N_EXPERTS * COMBINE_WIN), BF16),
                pltpu.VMEM((tm, D_MODEL), F32),
                pltpu.VMEM((GROUP_ROWS // 2, D_MODEL), BF16),
                pltpu.SemaphoreType.DMA((2,)),
                pltpu.SemaphoreType.DMA(()),
            ],
        ),
        out_shape=jax.ShapeDtypeStruct((T, D_MODEL), F32),
        compiler_params=_cparams(("arbitrary",), 48),
        name="combine",
    )(i32(win), i32(fast), i32(bounds), growt, ye, x1, mod3, g_post2)


def _rope_tables(n):
    rows = n // GRID_W
    row = jnp.repeat(jnp.arange(rows), GRID_W).astype(F32)
    col = jnp.tile(jnp.arange(GRID_W), rows).astype(F32)
    inv = ROPE_BASE ** (-jnp.arange(0, HD // 2, 2, dtype=F32) / (HD // 2))
    ang = jnp.concatenate([row[:, None] * inv, col[:, None] * inv], axis=-1)
    cos, sin = jnp.cos(ang), jnp.sin(ang)
    return jnp.tile(cos, (1, 4)), jnp.tile(jnp.concatenate([-sin, sin], axis=-1), (1, 2))


def kernel(x_prompt, x_sample, cache_k, cache_v, state_rnn, c, c_ctx, w_mod, b_mod, g_pre1, g_post1, g_pre2, g_post2, w_in, conv_w, conv_b, lru_wr, lru_br, lru_wi, lru_bi, lru_lam, lam_q1, lam_k1, lam_q2, lam_k2, g_sub, w_a_out, w_b_out, w_o, w_router, w_e_gate, w_e_up, w_e_down):
    nbc, nc, _ = x_prompt.shape
    nbl, nl, _ = x_sample.shape
    assert nbc * nc == GROUP_TOKENS and nl == GROUP_TOKENS and w_mod.shape[0] == 1
    tc = nbc * nc
    x_c, x_l = x_prompt.reshape(tc, D_MODEL), x_sample.reshape(nbl * nl, D_MODEL)
    sq0 = lambda a: a.reshape(a.shape[1:])
    sq1 = lambda a: a.reshape(a.shape[:1] + a.shape[2:])
    (w_mod, w_in, conv_w, lru_wr, lru_wi, lru_br, lru_bi, lru_lam, w_a_out, w_b_out, w_o, w_router, w_e_gate, w_e_up,
     w_e_down) = map(sq0, (w_mod, w_in, conv_w, lru_wr, lru_wi, lru_br, lru_bi, lru_lam, w_a_out, w_b_out, w_o,
                           w_router, w_e_gate, w_e_up, w_e_down))
    state_rnn, cache_k, cache_v = map(sq1, (state_rnn, cache_k, cache_v))

    cvec = jnp.concatenate([c_ctx[None, :], c, jnp.zeros((SUBLANES - 1 - nbl, D_MODEL), F32)], axis=0)
    mod = _modulation(cvec, w_mod, b_mod)
    mod3 = mod.reshape(SUBLANES, 1, 6 * D_MODEL)

    proj = _in_proj(x_c, x_l, mod3, g_pre1, w_in.astype(BF16))

    rnn_w = (conv_w, conv_b, lru_wr, lru_wi, lru_br, lru_bi, lru_lam)
    ya_c, h_fin = _rnn_branch(proj, *rnn_w, jnp.zeros((nbc, 2, D_RNN), F32), seq=nc, nseq=SUBLANES,
                              nb=nbc // SUBLANES, row0=0)
    ya_l, _ = _rnn_branch(proj, *rnn_w, state_rnn, seq=nl, nseq=1, nb=nbl, row0=tc // nl)

    lam_vecs = (lam_q1, lam_k1, lam_q2, lam_k2)
    on_c = _attention(proj, lam_vecs, g_sub, n=nc, nb=nbc, row0=0, tq=nc)
    past = cache_k.shape[1]
    on_l = _attention(proj, lam_vecs, g_sub, n=nl, nb=nbl, row0=tc // nl, tq=512,
                      cache=(cache_k.reshape(nbl, past, N_HEADS * VD), cache_v.reshape(nbl, past, N_HEADS * VD)),
                      tables=_rope_tables(nl))

    x1, h2, aff_t = _post_mix(ya_c, ya_l, on_c, on_l, proj, x_c, x_l, mod3, g_post1, g_pre2, w_a_out.astype(BF16),
                              w_b_out.astype(BF16), w_o.astype(BF16), w_router.T)

    capc = EC_FACTOR * nc // N_EXPERTS
    idx_c, val_c, growt_c = _route(aff_t, n=nc, nsets=nbc, col0=0, row_stride=capc)
    idx_l, val_l, growt_l = _route(aff_t, n=nl, nsets=nbl, col0=tc // nl, row_stride=0)
    gid_c = idx_c[..., 0] + (jnp.arange(nbc, dtype=jnp.int32) * nc)[:, None, None]
    gid_l = idx_l[..., 0] + (tc + jnp.arange(nbl, dtype=jnp.int32) * nl)[:, None, None]
    gidx = jnp.concatenate([gid_c.transpose(1, 0, 2).reshape(N_EXPERTS, -1),
                            gid_l.transpose(1, 0, 2).reshape(N_EXPERTS, -1)], axis=1)
    vals = jnp.concatenate([val_c.transpose(1, 0, 2, 3).reshape(N_EXPERTS, -1, 1),
                            val_l.transpose(1, 0, 2, 3).reshape(N_EXPERTS, -1, 1)], axis=1)
    growt = jnp.concatenate([growt_c, growt_l], axis=0)

    ye = _experts(gidx, h2, vals, w_e_gate.astype(BF16), w_e_up.astype(BF16), w_e_down.astype(BF16))
    x2 = _combine(growt, ye, x1, mod3, g_post2)

    y_prompt = x2[:tc].reshape(nbc, nc, D_MODEL)
    y_sample = x2[tc:].reshape(nbl, nl, D_MODEL)
    kblk = (2 * D_RNN + N_HEADS * VD) // (N_HEADS * VD)
    new_k = _take_cols(proj, tc, kblk, N_HEADS * VD).reshape(nbc, 1, nc, N_HEADS, 2 * HD)
    new_v = _take_cols(proj, tc, kblk + 1, N_HEADS * VD).reshape(nbc, 1, nc, N_HEADS, VD)
    return (y_prompt, y_sample, new_k, new_v, h_fin[:, None])
```

```python
import functools
import math

import jax
import jax.numpy as jnp
from jax import lax
from jax.experimental import pallas as pl
from jax.experimental.pallas import tpu as pltpu

F32 = jnp.float32
BF16 = jnp.bfloat16
HIGHEST = lax.Precision.HIGHEST

D_MODEL = 2048
D_RNN = 1024
RNN_BLOCKS = 8
RNN_BS = 128
CONV_W = 4
LRU_C = 8.0
N_HEADS = 8
HD = 64
VD = 128
N_EXPERTS = 16
D_EXPERT = 1024
EC_FACTOR = 2
EPS = 1e-6
IN_COLS = 9216
GRID_W = 64
ROPE_BASE = 10000.0
LAM_INIT = 0.8 - 0.6 * math.exp(-0.3 * 0)

GROUP_TOKENS = 4096
GROUP_ROWS = 512
SUBLANES = 8
LANES = 128
V7X_VMEM_BYTES = 64 * 1024 * 1024


def _cparams(semantics, vmem_mb):
    return pltpu.CompilerParams(dimension_semantics=semantics, vmem_limit_bytes=vmem_mb * 1024 * 1024)


def _sigmoid(x):
    return 0.5 * jnp.tanh(0.5 * x) + 0.5


def _rms(x, g):
    return x * lax.rsqrt(jnp.mean(x * x, axis=-1, keepdims=True) + EPS) * g


def _mod_kernel(c_ref, w_ref, b_ref, o_ref):
    c = c_ref[...]
    s = c * _sigmoid(c)
    o_ref[...] = jnp.dot(s, w_ref[...], precision=HIGHEST, preferred_element_type=F32) + b_ref[...]


def _modulation(cvec, w_mod, b_mod):
    tn = 1024
    return pl.pallas_call(
        _mod_kernel,
        grid=(6 * D_MODEL // tn,),
        in_specs=[
            pl.BlockSpec((SUBLANES, D_MODEL), lambda j: (0, 0)),
            pl.BlockSpec((D_MODEL, tn), lambda j: (0, j)),
            pl.BlockSpec((1, tn), lambda j: (0, j)),
        ],
        out_specs=pl.BlockSpec((SUBLANES, tn), lambda j: (0, j)),
        out_shape=jax.ShapeDtypeStruct((SUBLANES, 6 * D_MODEL), F32),
        compiler_params=_cparams(("arbitrary",), 40),
        name="modulation",
    )(cvec, w_mod, b_mod)


def _in_kernel(xc_ref, xl_ref, sh_ref, sc_ref, g_ref, w_ref, o_ref, h_ref, *, ctx_tiles):
    i, j = pl.program_id(0), pl.program_id(1)
    rows = 128

    def prologue(x_ref):
        def body(c, carry):
            r0 = pl.multiple_of(c * rows, rows)
            h = _rms(x_ref[pl.ds(r0, rows), :], g_ref[...]) * (1.0 + sc_ref[...]) + sh_ref[...]
            h_ref[pl.ds(r0, rows), :] = h.astype(BF16)
            return carry

        lax.fori_loop(0, h_ref.shape[0] // rows, body, 0)

    @pl.when((j == 0) & (i < ctx_tiles))
    def _():
        prologue(xc_ref)

    @pl.when((j == 0) & (i >= ctx_tiles))
    def _():
        prologue(xl_ref)

    o_ref[...] = jnp.dot(h_ref[...], w_ref[...], preferred_element_type=F32)


def _in_proj(x_c, x_l, mod3, g_pre1, w_in_bf):
    T = x_c.shape[0] + x_l.shape[0]
    tm, tn = 1024, 1024
    tpg = GROUP_TOKENS // tm
    nct = x_c.shape[0] // tm
    return pl.pallas_call(
        functools.partial(_in_kernel, ctx_tiles=nct),
        grid=(T // tm, IN_COLS // tn),
        in_specs=[
            pl.BlockSpec((tm, D_MODEL), lambda i, j: (jnp.minimum(i, nct - 1), 0), pipeline_mode=pl.Buffered(1)),
            pl.BlockSpec((tm, D_MODEL), lambda i, j: (jnp.maximum(i - nct, 0), 0)),
            pl.BlockSpec((None, 1, D_MODEL), lambda i, j: (i // tpg, 0, 0)),
            pl.BlockSpec((None, 1, D_MODEL), lambda i, j: (i // tpg, 0, 1)),
            pl.BlockSpec((1, D_MODEL), lambda i, j: (0, 0)),
            pl.BlockSpec((D_MODEL, tn), lambda i, j: (0, j)),
        ],
        out_specs=pl.BlockSpec((tm, tn), lambda i, j: (i, j)),
        out_shape=jax.ShapeDtypeStruct((T, IN_COLS), F32),
        scratch_shapes=[pltpu.VMEM((tm, D_MODEL), BF16)],
        compiler_params=_cparams(("arbitrary", "arbitrary"), 56),
        name="in_proj",
    )(x_c, x_l, mod3, mod3, g_pre1, w_in_bf)


def _copy_kernel(x_ref, o_ref):
    o_ref[...] = x_ref[...]


def _take_cols(proj, rows, col_block, width):
    tm = 1024
    return pl.pallas_call(
        _copy_kernel,
        grid=(rows // tm,),
        in_specs=[pl.BlockSpec((tm, width), lambda i: (i, col_block))],
        out_specs=pl.BlockSpec((tm, width), lambda i: (i, 0)),
        out_shape=jax.ShapeDtypeStruct((rows, width), proj.dtype),
        compiler_params=_cparams(("arbitrary",), 40),
        name="take_cols",
    )(proj)


def _gelu_tanh(x):
    return 0.5 * x * (1.0 + jnp.tanh(math.sqrt(2.0 / math.pi) * (x + 0.044715 * (x * x * x))))


def _softplus(x):
    return jnp.maximum(x, 0.0) + jnp.log(1.0 + jnp.exp(-jnp.abs(x)))


def _split_bf16(x):
    hi = x.astype(BF16)
    return hi, (x - hi.astype(F32)).astype(BF16)


def _dot_3pass(x, w_hi, w_lo):
    x_hi, x_lo = _split_bf16(x)
    d = functools.partial(jnp.dot, preferred_element_type=F32)
    return d(x_hi, w_hi) + (d(x_lo, w_hi) + d(x_hi, w_lo))


def _rnn_kernel(xr_ref, gr_ref, cw_ref, cb_ref, wr_ref, wi_ref, br_ref, bi_ref, lam_ref, h0_ref,
                ya_ref, hf_ref, xp_ref, a_ref, b_ref, h_ref, p_ref, *, seq, nseq):
    n = seq * nseq
    seg = n // SUBLANES
    pitch = seg + SUBLANES
    zeros8 = jnp.zeros((SUBLANES, LANES), F32)
    xp_ref[0:SUBLANES, :] = zeros8
    xp_ref[SUBLANES + n:2 * SUBLANES + n, :] = zeros8
    xp_ref[SUBLANES:SUBLANES + n, :] = xr_ref[...]
    cw = cw_ref[...]
    cb = cb_ref[...]
    sp = _softplus(-lam_ref[...])
    w_r = [_split_bf16(wr_ref[d]) for d in range(2)]
    w_i = [_split_bf16(wi_ref[d]) for d in range(2)]

    def gates(s, carry):
        r0 = pl.multiple_of(s * seg, SUBLANES)
        taps = [xp_ref[pl.ds(r0 + 6 + k, seg), :] for k in range(CONV_W)]
        if nseq > 1:
            t = (r0 + lax.broadcasted_iota(jnp.int32, (seg, LANES), 0)) % seq
            taps[0] = jnp.where(t >= 2, taps[0], 0.0)
            taps[1] = jnp.where(t >= 1, taps[1], 0.0)
            taps[3] = jnp.where(t <= seq - 2, taps[3], 0.0)
        xc = cb + taps[0] * cw[0:1]
        xc = xc + taps[1] * cw[1:2]
        xc = xc + taps[2] * cw[2:3]
        xc = xc + taps[3] * cw[3:4]
        p0 = pl.multiple_of(s * pitch, SUBLANES)
        for d in range(2):
            r = _sigmoid(_dot_3pass(xc, *w_r[d]) + br_ref[d:d + 1, :])
            i = _sigmoid(_dot_3pass(xc, *w_i[d]) + bi_ref[d:d + 1, :])
            a = jnp.exp((-LRU_C) * r * sp[d:d + 1, :])
            y = 1.0 - a * a
            a_ref[d, pl.ds(p0, seg), :] = a
            b_ref[d, pl.ds(p0, seg), :] = jnp.where(y > 0.0, y * lax.rsqrt(y), 0.0) * (i * xc)
        return carry

    lax.fori_loop(0, SUBLANES, gates, 0)

    def scan(j, carry):
        hf, af, hb, ab = carry
        a0 = a_ref[0, pl.ds(j, SUBLANES, stride=pitch), :]
        b0 = b_ref[0, pl.ds(j, SUBLANES, stride=pitch), :]
        hf = a0 * hf + b0
        af = a0 * af
        h_ref[0, pl.ds(j, SUBLANES, stride=pitch), :] = hf
        p_ref[0, pl.ds(j, SUBLANES, stride=pitch), :] = af
        jb = seg - 1 - j
        a1 = a_ref[1, pl.ds(jb, SUBLANES, stride=pitch), :]
        b1 = b_ref[1, pl.ds(jb, SUBLANES, stride=pitch), :]
        hb = a1 * hb + b1
        ab = a1 * ab
        h_ref[1, pl.ds(jb, SUBLANES, stride=pitch), :] = hb
        p_ref[1, pl.ds(jb, SUBLANES, stride=pitch), :] = ab
        return hf, af, hb, ab

    ones8 = jnp.ones((SUBLANES, LANES), F32)
    ef, pf, eb, pb = lax.fori_loop(0, seg, scan, (zeros8, ones8, zeros8, ones8), unroll=8)

    spq = SUBLANES // nseq
    cf = [None] * SUBLANES
    cbk = [None] * SUBLANES
    for q in range(nseq):
        h0 = h0_ref[q]
        first, last = q * spq, q * spq + spq - 1
        cf[first] = h0[0:1, :]
        for s in range(first + 1, last + 1):
            cf[s] = ef[s - 1:s, :] + pf[s - 1:s, :] * cf[s - 1]
        cbk[last] = h0[1:2, :]
        for s in range(last - 1, first - 1, -1):
            cbk[s] = eb[s + 1:s + 2, :] + pb[s + 1:s + 2, :] * cbk[s + 1]
        hf_ref[q, 0:1, :] = ef[last:last + 1, :] + pf[last:last + 1, :] * cf[last]
        hf_ref[q, 1:2, :] = eb[first:first + 1, :] + pb[first:first + 1, :] * cbk[first]

    for s in range(SUBLANES):
        hfw = h_ref[0, s * pitch:s * pitch + seg, :] + p_ref[0, s * pitch:s * pitch + seg, :] * cf[s]
        hbw = h_ref[1, s * pitch:s * pitch + seg, :] + p_ref[1, s * pitch:s * pitch + seg, :] * cbk[s]
        y = (hfw + hbw) * _gelu_tanh(gr_ref[s * seg:(s + 1) * seg, :])
        ya_ref[s * seg:(s + 1) * seg, :] = y.astype(ya_ref.dtype)


def _rnn_branch(proj, conv_w, conv_b, wr, wi, br, bi, lam, h0, *, seq, nseq, nb, row0):
    n = seq * nseq
    seg = n // SUBLANES
    pitch = seg + SUBLANES
    return pl.pallas_call(
        functools.partial(_rnn_kernel, seq=seq, nseq=nseq),
        grid=(nb, RNN_BLOCKS),
        in_specs=[
            pl.BlockSpec((n, RNN_BS), lambda b, k: (row0 + b, k)),
            pl.BlockSpec((n, RNN_BS), lambda b, k: (row0 + b, RNN_BLOCKS + k)),
            pl.BlockSpec((CONV_W, RNN_BS), lambda b, k: (0, k)),
            pl.BlockSpec((1, RNN_BS), lambda b, k: (0, k)),
            pl.BlockSpec((2, None, RNN_BS, RNN_BS), lambda b, k: (0, k, 0, 0)),
            pl.BlockSpec((2, None, RNN_BS, RNN_BS), lambda b, k: (0, k, 0, 0)),
            pl.BlockSpec((2, RNN_BS), lambda b, k: (0, k)),
            pl.BlockSpec((2, RNN_BS), lambda b, k: (0, k)),
            pl.BlockSpec((2, RNN_BS), lambda b, k: (0, k)),
            pl.BlockSpec((nseq, 2, RNN_BS), lambda b, k: (b, 0, k)),
        ],
        out_specs=[
            pl.BlockSpec((n, RNN_BS), lambda b, k: (b, k)),
            pl.BlockSpec((nseq, 2, RNN_BS), lambda b, k: (b, 0, k)),
        ],
        out_shape=[jax.ShapeDtypeStruct((nb * n, D_RNN), BF16), jax.ShapeDtypeStruct((nb * nseq, 2, D_RNN), F32)],
        scratch_shapes=[
            pltpu.VMEM((n + 2 * SUBLANES, RNN_BS), F32),
        ] + [pltpu.VMEM((2, SUBLANES * pitch, RNN_BS), F32)] * 4,
        compiler_params=_cparams(("arbitrary", "arbitrary"), 40),
        name=f"rnn_n{n}",
    )(proj, proj, conv_w, conv_b, wr, wi, br, bi, lam, h0)


def _rope(x, cos, sin_signed):
    lane = lax.broadcasted_iota(jnp.int32, x.shape, 1)
    partner = jnp.where((lane % HD) < HD // 2, pltpu.roll(x, LANES - HD // 2, 1), pltpu.roll(x, HD // 2, 1))
    return x * cos + partner * sin_signed


ATTN_SUB = 128


def _attn_kernel(*refs, n, npast, tq, rope):
    if rope:
        (lq1, lk1, lq2, lk2, q_ref, k_ref, v_ref, gsub_ref, ck_ref, cv_ref, cosk_ref, sink_ref, cosq_ref, sinq_ref,
         o_ref, kb_ref, vb_ref, s_ref, p_ref) = refs
    else:
        lq1, lk1, lq2, lk2, q_ref, k_ref, v_ref, gsub_ref, o_ref, kb_ref, vb_ref, s_ref, p_ref = refs
    chunk = 256
    nk = n + npast

    @pl.when(pl.program_id(2) == 0)
    def _():
        ones_col = jnp.where(lax.broadcasted_iota(jnp.int32, (chunk, VD), 1) == 0, 1.0, 0.0).astype(BF16)
        if npast:
            for r0 in range(0, npast, chunk):
                kb_ref[:, r0:r0 + chunk] = ck_ref[r0:r0 + chunk, :].T.astype(BF16)
                vb_ref[r0:r0 + chunk, VD:2 * VD] = ones_col
            vb_ref[0:npast, 0:VD] = cv_ref[...].astype(BF16)

        for r0 in range(0, n, chunk):
            k = k_ref[r0:r0 + chunk, :]
            if rope:
                k = _rope(k, cosk_ref[r0:r0 + chunk, :], sink_ref[r0:r0 + chunk, :])
            kb_ref[:, npast + r0:npast + r0 + chunk] = k.T.astype(BF16)
            vb_ref[npast + r0:npast + r0 + chunk, 0:VD] = v_ref[r0:r0 + chunk, :].astype(BF16)
            vb_ref[npast + r0:npast + r0 + chunk, VD:2 * VD] = ones_col

    lam = (jnp.exp(jnp.sum(lq1[...] * lk1[...], axis=-1, keepdims=True))
           - jnp.exp(jnp.sum(lq2[...] * lk2[...], axis=-1, keepdims=True)) + LAM_INIT)
    q = q_ref[...]
    if rope:
        q = _rope(q, cosq_ref[...], sinq_ref[...])
    q = q * (HD ** -0.5 * math.log2(math.e))
    lane = lax.broadcasted_iota(jnp.int32, (ATTN_SUB, VD), 1)
    nt = (((1,), (1,)), ((), ()))
    kc = 2 * LANES
    key_chunks = [(c0, min(kc, nk - c0)) for c0 in range(0, nk, kc)]
    for sub in range(tq // ATTN_SUB):
        qs = q[sub * ATTN_SUB:(sub + 1) * ATTN_SUB, :]
        outs = []
        for comp in range(2):
            unit = 2 * sub + comp
            qc = jnp.where((lane < HD) == (comp == 0), qs, 0.0).astype(BF16)
            m_run = None
            for c0, w in key_chunks:
                s = jnp.dot(qc, kb_ref[:, c0:c0 + w], preferred_element_type=F32)
                s_ref[unit, :, c0:c0 + w] = s
                for l0 in range(0, w, LANES):
                    piece = s[:, l0:l0 + LANES]
                    m_run = piece if m_run is None else jnp.maximum(m_run, piece)
            m = jnp.max(m_run, axis=-1, keepdims=True)
            for c0, w in key_chunks:
                p_ref[unit, :, c0:c0 + w] = jnp.exp2(s_ref[unit, :, c0:c0 + w] - m).astype(BF16)
            outs.append(jnp.dot(p_ref[unit], vb_ref[...], preferred_element_type=F32))
        o = (outs[0][:, 0:VD] * (1.0 / outs[0][:, VD:VD + 1])
             - outs[1][:, 0:VD] * (lam / outs[1][:, VD:VD + 1]))
        o = _rms(o, gsub_ref[...]) * (1.0 - LAM_INIT)
        o_ref[sub * ATTN_SUB:(sub + 1) * ATTN_SUB, :] = o.astype(o_ref.dtype)


def _attention(proj, lam_vecs, g_sub, *, n, nb, row0, tq, cache=None, tables=None):
    rope = tables is not None
    npast = 0 if cache is None else cache[0].shape[1]
    nk = n + npast
    qpr = n // tq
    qcol, kcol, vcol = 2 * RNN_BLOCKS, 2 * RNN_BLOCKS + N_HEADS, 2 * RNN_BLOCKS + 2 * N_HEADS
    vec = pl.BlockSpec((1, HD), lambda b, h, t: (0, 0))
    in_specs = [vec, vec, vec, vec,
                pl.BlockSpec((tq, VD), lambda b, h, t: ((row0 + b) * qpr + t, qcol + h)),
                pl.BlockSpec((n, VD), lambda b, h, t: (row0 + b, kcol + h)),
                pl.BlockSpec((n, VD), lambda b, h, t: (row0 + b, vcol + h)),
                pl.BlockSpec((1, VD), lambda b, h, t: (0, 0))]
    args = [*lam_vecs, proj, proj, proj, g_sub]
    if rope:
        ck, cv = cache
        cos_t, sin_t = tables
        in_specs += [pl.BlockSpec((None, npast, VD), lambda b, h, t: (b, 0, h)),
                     pl.BlockSpec((None, npast, VD), lambda b, h, t: (b, 0, h)),
                     pl.BlockSpec((n, VD), lambda b, h, t: (0, 0), pipeline_mode=pl.Buffered(1)),
                     pl.BlockSpec((n, VD), lambda b, h, t: (0, 0), pipeline_mode=pl.Buffered(1)),
                     pl.BlockSpec((tq, VD), lambda b, h, t: (t, 0)),
                     pl.BlockSpec((tq, VD), lambda b, h, t: (t, 0))]
        args += [ck, cv, cos_t, sin_t, cos_t, sin_t]
    return pl.pallas_call(
        functools.partial(_attn_kernel, n=n, npast=npast, tq=tq, rope=rope),
        grid=(nb, N_HEADS, qpr),
        in_specs=in_specs,
        out_specs=pl.BlockSpec((tq, VD), lambda b, h, t: (b * qpr + t, h)),
        out_shape=jax.ShapeDtypeStruct((nb * n, N_HEADS * VD), BF16),
        scratch_shapes=[pltpu.VMEM((VD, nk), BF16), pltpu.VMEM((nk, 2 * VD), BF16),
                        pltpu.VMEM((2 * tq // ATTN_SUB, ATTN_SUB, nk), F32),
                        pltpu.VMEM((2 * tq // ATTN_SUB, ATTN_SUB, nk), BF16)],
        compiler_params=_cparams(("arbitrary", "arbitrary", "arbitrary"), 56),
        name=f"attn_n{n}",
    )(*args)


POST_TM = 512
POST_UNIT = 256


def _merge_kernel(yac_ref, yal_ref, onc_ref, onl_ref, ga0_ref, ga1_ref, gb0_ref, gb1_ref, wa_ref, wb_ref, m_ref,
                  *, ctx_tiles):
    is_ctx = pl.program_id(0) < ctx_tiles
    half = D_MODEL // 2
    for r0 in range(0, POST_TM, POST_UNIT):
        rs = slice(r0, r0 + POST_UNIT)
        ya = jnp.where(is_ctx, yac_ref[rs, :], yal_ref[rs, :])
        on = jnp.where(is_ctx, onc_ref[rs, :], onl_ref[rs, :])
        oa = jnp.dot(ya, wa_ref[...], preferred_element_type=F32)
        ob = jnp.dot(on, wb_ref[...], preferred_element_type=F32)
        m_ref[rs, 0:half] = (_sigmoid(ga0_ref[rs, :]) * oa[:, 0:half]
                             + _sigmoid(gb0_ref[rs, :]) * ob[:, 0:half]).astype(BF16)
        m_ref[rs, half:] = (_sigmoid(ga1_ref[rs, :]) * oa[:, half:]
                            + _sigmoid(gb1_ref[rs, :]) * ob[:, half:]).astype(BF16)


def _outproj_kernel(m_ref, xc_ref, xl_ref, g1_ref, sh2_ref, sc2_ref, gpost1_ref, gpre2_ref, wo_ref, wrt_ref,
                    x1_ref, h2_ref, aff_ref, *, ctx_tiles):
    is_ctx = pl.program_id(0) < ctx_tiles
    nt = functools.partial(lax.dot_general, dimension_numbers=(((1,), (1,)), ((), ())), preferred_element_type=F32)
    w_hi, w_lo = _split_bf16(wrt_ref[...])
    units = [slice(r0, r0 + POST_UNIT) for r0 in range(0, POST_TM, POST_UNIT)]
    ts = [jnp.dot(m_ref[rs, :], wo_ref[...], preferred_element_type=F32) for rs in units]
    for rs, t in zip(units, ts):
        x1 = jnp.where(is_ctx, xc_ref[rs, :], xl_ref[rs, :]) + g1_ref[...] * _rms(t, gpost1_ref[...])
        x1_ref[rs, :] = x1
        h2 = _rms(x1, gpre2_ref[...]) * (1.0 + sc2_ref[...]) + sh2_ref[...]
        h2_ref[rs, :] = h2
        h_hi, h_lo = _split_bf16(h2)
        logits = nt(w_hi, h_hi) + (nt(w_lo, h_hi) + nt(w_hi, h_lo))
        e = jnp.exp(logits - jnp.max(logits, axis=0, keepdims=True))
        aff_ref[:, rs] = e / jnp.sum(e, axis=0, keepdims=True)


def _post_mix(ya_c, ya_l, on_c, on_l, proj, x_c, x_l, mod3, g_post1, g_pre2, wa, wb, wo, wrt):
    T = x_c.shape[0] + x_l.shape[0]
    tm = POST_TM
    tpg = GROUP_TOKENS // tm
    nct = ya_c.shape[0] // tm
    ctx_blk = lambda w: pl.BlockSpec((tm, w), lambda i: (jnp.minimum(i, nct - 1), 0), pipeline_mode=pl.Buffered(1))
    lat_blk = lambda w: pl.BlockSpec((tm, w), lambda i: (jnp.maximum(i - nct, 0), 0))
    half = D_MODEL // 2
    gate0 = (2 * D_RNN + 2 * N_HEADS * VD + N_HEADS * VD) // half
    row = lambda c: pl.BlockSpec((None, 1, D_MODEL), lambda i: (i // tpg, 0, c))
    const = lambda shape: pl.BlockSpec(shape, lambda i: (0,) * len(shape), pipeline_mode=pl.Buffered(1))
    merged = pl.pallas_call(
        functools.partial(_merge_kernel, ctx_tiles=nct),
        grid=(T // tm,),
        in_specs=[
            ctx_blk(D_RNN), lat_blk(D_RNN), ctx_blk(N_HEADS * VD), lat_blk(N_HEADS * VD),
            pl.BlockSpec((tm, half), lambda i: (i, gate0)),
            pl.BlockSpec((tm, half), lambda i: (i, gate0 + 1)),
            pl.BlockSpec((tm, half), lambda i: (i, gate0 + 2)),
            pl.BlockSpec((tm, half), lambda i: (i, gate0 + 3)),
            const((D_RNN, D_MODEL)), const((N_HEADS * VD, D_MODEL)),
        ],
        out_specs=pl.BlockSpec((tm, D_MODEL), lambda i: (i, 0)),
        out_shape=jax.ShapeDtypeStruct((T, D_MODEL), BF16),
        compiler_params=_cparams(("arbitrary",), 56),
        name="merge",
    )(ya_c, ya_l, on_c, on_l, proj, proj, proj, proj, wa, wb)
    return pl.pallas_call(
        functools.partial(_outproj_kernel, ctx_tiles=nct),
        grid=(T // tm,),
        in_specs=[
            pl.BlockSpec((tm, D_MODEL), lambda i: (i, 0)),
            ctx_blk(D_MODEL), lat_blk(D_MODEL),
            row(2), row(3), row(4),
            const((1, D_MODEL)), const((1, D_MODEL)),
            const((D_MODEL, D_MODEL)),
            const((N_EXPERTS, D_MODEL)),
        ],
        out_specs=[
            pl.BlockSpec((tm, D_MODEL), lambda i: (i, 0)),
            pl.BlockSpec((tm, D_MODEL), lambda i: (i, 0)),
            pl.BlockSpec((N_EXPERTS, tm), lambda i: (0, i)),
        ],
        out_shape=[jax.ShapeDtypeStruct((T, D_MODEL), F32), jax.ShapeDtypeStruct((T, D_MODEL), F32),
                   jax.ShapeDtypeStruct((N_EXPERTS, T), F32)],
        compiler_params=_cparams(("arbitrary",), 56),
        name="out_proj",
    )(merged, x_c, x_l, mod3, mod3, mod3, g_post1, g_pre2, wo, wrt)


def _cumsum_excl(x, tri):
    n = x.shape[1]
    blk = tri.shape[0]
    outs = []
    carry = jnp.zeros((x.shape[0], 1), F32)
    for c in range(n // blk):
        xc = x[:, c * blk:(c + 1) * blk]
        outs.append(jnp.dot(xc.astype(BF16), tri, preferred_element_type=F32) + carry)
        carry = carry + jnp.sum(xc, axis=1, keepdims=True)
    return outs[0] if len(outs) == 1 else jnp.concatenate(outs, axis=1)


def _route_kernel(aff_ref, idx_ref, val_ref, growt_ref, *, n, cap, row_stride):
    aff = aff_ref[...]

    def bisect(_, lohi):
        lo, hi = lohi
        mid = lo + ((hi - lo) >> 1)
        cnt = jnp.sum((aff >= pltpu.bitcast(mid, F32)).astype(jnp.int32), axis=1, keepdims=True)
        ok = cnt >= cap
        return jnp.where(ok, mid, lo), jnp.where(ok, hi, mid)

    lo0 = jnp.zeros((N_EXPERTS, 1), jnp.int32)
    hi0 = jnp.full((N_EXPERTS, 1), 0x7F800000, jnp.int32)
    thr, _ = lax.fori_loop(0, 31, bisect, (lo0, hi0))
    gt = aff >= pltpu.bitcast(thr + 1, F32)
    eq = (aff >= pltpu.bitcast(thr, F32)) & jnp.logical_not(gt)
    need = cap - jnp.sum(gt.astype(jnp.int32), axis=1, keepdims=True)
    blk = min(n, 256)
    tri = (lax.broadcasted_iota(jnp.int32, (blk, blk), 0) < lax.broadcasted_iota(jnp.int32, (blk, blk), 1)).astype(BF16)
    eqpos = _cumsum_excl(eq.astype(F32), tri)
    sel = gt | (eq & (eqpos < need.astype(F32)))
    pos = _cumsum_excl(sel.astype(F32), tri).astype(jnp.int32)
    slot = jnp.where(sel, pos, -1)
    grow = jnp.where(sel, pos + pl.program_id(0) * row_stride, -1)
    growt_ref[...] = grow.astype(F32).T.astype(jnp.int32)

    rc = min(cap, 32)
    tok = lax.broadcasted_iota(jnp.int32, (rc, n), 1)
    for e in range(N_EXPERTS):
        slot_e = slot[e:e + 1, :]
        aff_e = aff[e:e + 1, :]

        def extract(c, carry):
            r0 = pl.multiple_of(c * rc, rc)
            hit = slot_e == (lax.broadcasted_iota(jnp.int32, (rc, n), 0) + r0)
            idx_ref[e, pl.ds(r0, rc), :] = jnp.sum(jnp.where(hit, tok, 0), axis=1, keepdims=True)
            val_ref[e, pl.ds(r0, rc), :] = jnp.sum(jnp.where(hit, aff_e, 0.0), axis=1, keepdims=True)
            return carry

        lax.fori_loop(0, cap // rc, extract, 0)


def _route(aff_t, *, n, nsets, col0, row_stride):
    cap = EC_FACTOR * n // N_EXPERTS
    kern = functools.partial(_route_kernel, n=n, cap=cap, row_stride=row_stride)
    return pl.pallas_call(
        kern,
        grid=(nsets,),
        in_specs=[pl.BlockSpec((N_EXPERTS, n), lambda s: (0, col0 + s))],
        out_specs=[
            pl.BlockSpec((None, N_EXPERTS, cap, 1), lambda s: (s, 0, 0, 0)),
            pl.BlockSpec((None, N_EXPERTS, cap, 1), lambda s: (s, 0, 0, 0)),
            pl.BlockSpec((n, N_EXPERTS), lambda s: (s, 0)),
        ],
        out_shape=[jax.ShapeDtypeStruct((nsets, N_EXPERTS, cap, 1), jnp.int32),
                   jax.ShapeDtypeStruct((nsets, N_EXPERTS, cap, 1), F32),
                   jax.ShapeDtypeStruct((nsets * n, N_EXPERTS), jnp.int32)],
        compiler_params=_cparams(("arbitrary",), 48),
        name=f"route_n{n}",
    )(aff_t)


WEIGHT_PARTS = 4


def _expert_kernel(cur_ref, nxt_ref, h2_hbm, val_ref, wg_hbm, wu_hbm, wd_hbm, ye_ref, xs0_ref, xs1_ref,
                   wg_ref, wu_ref, wd_ref, sg_ref, su_ref, sd_ref, sem, wsem, *, rows):
    e, grp = pl.program_id(0), pl.program_id(1)
    step = e * pl.num_programs(1) + grp
    nsteps = pl.num_programs(0) * pl.num_programs(1)
    bufs = (xs0_ref, xs1_ref)
    rin, rdn = D_MODEL // WEIGHT_PARTS, D_EXPERT // WEIGHT_PARTS

    def slab_copies(ex, part):
        r_in = pl.multiple_of(part * rin, rin)
        r_dn = pl.multiple_of(part * rdn, rdn)
        return (pltpu.make_async_copy(wg_hbm.at[ex, pl.ds(r_in, rin), :], sg_ref, wsem.at[0]),
                pltpu.make_async_copy(wu_hbm.at[ex, pl.ds(r_in, rin), :], su_ref, wsem.at[1]),
                pltpu.make_async_copy(wd_hbm.at[ex, pl.ds(r_dn, rdn), :], sd_ref, wsem.at[2]))

    def fetch(ex, part):
        for cp in slab_copies(ex, part):
            cp.start()

    def land(ex, part):
        for cp in slab_copies(ex, part):
            cp.wait()
        st = ex % 2
        wg_ref[st, pl.ds(pl.multiple_of(part * rin, rin), rin), :] = sg_ref[...].astype(BF16)
        wu_ref[st, pl.ds(pl.multiple_of(part * rin, rin), rin), :] = su_ref[...].astype(BF16)
        wd_ref[st, pl.ds(pl.multiple_of(part * rdn, rdn), rdn), :] = sd_ref[...].astype(BF16)

    @pl.when(step == 0)
    def _():
        for part in range(WEIGHT_PARTS):
            fetch(0, part)
            land(0, part)

    has_next = e + 1 < pl.num_programs(0)

    @pl.when(has_next & (grp >= 1) & (grp <= WEIGHT_PARTS))
    def _():
        land(e + 1, grp - 1)

    @pl.when(has_next & (grp < WEIGHT_PARTS))
    def _():
        fetch(e + 1, grp)

    wset = e % 2

    def row_copy(idx_ref, r, s):
        return pltpu.make_async_copy(h2_hbm.at[pl.ds(idx_ref[0, r], 1), :], bufs[s].at[pl.ds(r, 1), :], sem.at[s])

    def slot_wait(s):
        pltpu.make_async_copy(h2_hbm.at[pl.ds(0, rows), :], bufs[s], sem.at[s]).wait()

    @pl.when(step == 0)
    def _():
        def issue(r, carry):
            row_copy(cur_ref, r, 0).start()
            return carry

        lax.fori_loop(0, rows, issue, 0, unroll=8)

    def body(s):
        slot_wait(s)
        for r in range(rows):
            row_copy(nxt_ref, r, 1 - s).start()
        half = rows // 2
        for c in range(2):
            xs = bufs[s][c * half:(c + 1) * half, :].astype(BF16)
            g = jnp.dot(xs, wg_ref[wset], preferred_element_type=F32)
            u = jnp.dot(xs, wu_ref[wset], preferred_element_type=F32)
            hid = (g * _sigmoid(g)) * u
            y = jnp.dot(hid.astype(BF16), wd_ref[wset], preferred_element_type=F32) * val_ref[c * half:(c + 1) * half, :]
            ye_ref[c * half:(c + 1) * half, :] = y.astype(ye_ref.dtype)

        @pl.when(step == nsteps - 1)
        def _():
            slot_wait(1 - s)

    for s in range(2):
        pl.when(step % 2 == s)(functools.partial(body, s))


def _experts(gidx, h2, vals, wg, wu, wd):
    rows_per_expert = vals.shape[1]
    rows = GROUP_ROWS
    groups = rows_per_expert // rows
    assert groups > WEIGHT_PARTS
    last = N_EXPERTS * groups - 1
    hbm = pl.BlockSpec(memory_space=pl.ANY)
    return pl.pallas_call(
        functools.partial(_expert_kernel, rows=rows),
        grid=(N_EXPERTS, groups),
        in_specs=[
            pl.BlockSpec((None, 1, rows), lambda e, g: (e * groups + g, 0, 0), memory_space=pltpu.SMEM),
            pl.BlockSpec((None, 1, rows), lambda e, g: (jnp.minimum(e * groups + g + 1, last), 0, 0),
                         memory_space=pltpu.SMEM),
            hbm,
            pl.BlockSpec((None, rows, 1), lambda e, g: (e, g, 0)),
            hbm, hbm, hbm,
        ],
        out_specs=pl.BlockSpec((None, rows, D_MODEL), lambda e, g: (e, g, 0)),
        out_shape=jax.ShapeDtypeStruct((N_EXPERTS, rows_per_expert, D_MODEL), BF16),
        scratch_shapes=[pltpu.VMEM((rows, D_MODEL), F32), pltpu.VMEM((rows, D_MODEL), F32),
                        pltpu.VMEM((2, D_MODEL, D_EXPERT), BF16), pltpu.VMEM((2, D_MODEL, D_EXPERT), BF16),
                        pltpu.VMEM((2, D_EXPERT, D_MODEL), BF16),
                        pltpu.VMEM((D_MODEL // WEIGHT_PARTS, D_EXPERT), F32),
                        pltpu.VMEM((D_MODEL // WEIGHT_PARTS, D_EXPERT), F32),
                        pltpu.VMEM((D_EXPERT // WEIGHT_PARTS, D_MODEL), F32),
                        pltpu.SemaphoreType.DMA((2,)), pltpu.SemaphoreType.DMA((3,))],
        compiler_params=_cparams(("arbitrary", "arbitrary"), 56),
        name="experts",
    )(gidx.reshape(N_EXPERTS * groups, 1, rows), gidx.reshape(N_EXPERTS * groups, 1, rows), h2, vals, wg, wu, wd)


COMBINE_TM = 256
COMBINE_WIN = 128
BF16_ROW_TILE = 16


def _combine_kernel(win_ref, fast_ref, bounds_ref, growt_ref, ye_hbm, x1_ref, g2_ref, gpost2_ref, oc_ref, ol_ref,
                    wbuf_ref, oh_ref, acc_ref, sbuf_ref, wsem, ssem, *, ctx_steps):
    i = pl.program_id(0)
    nsteps = pl.num_programs(0)
    tpg = GROUP_TOKENS // COMBINE_TM
    slot = i % 2
    half = GROUP_ROWS // 2
    growt = growt_ref[...]

    def window_copy(step, e, s):
        start = pl.multiple_of((step // tpg) * GROUP_ROWS + win_ref[step * N_EXPERTS + e], BF16_ROW_TILE)
        return pltpu.make_async_copy(ye_hbm.at[e, pl.ds(start, COMBINE_WIN), :],
                                     wbuf_ref.at[s, pl.ds(e * COMBINE_WIN, COMBINE_WIN), :], wsem.at[s])

    def fetch(step, s):
        for e in range(N_EXPERTS):
            window_copy(step, e, s).start()

    @pl.when((i == 0) & (fast_ref[0] == 1))
    def _():
        fetch(0, 0)

    nxt = jnp.minimum(i + 1, nsteps - 1)

    @pl.when((i + 1 < nsteps) & (fast_ref[nxt] == 1))
    def _():
        fetch(nxt, 1 - slot)

    @pl.when(fast_ref[i] == 1)
    def _():
        for e in range(N_EXPERTS):
            window_copy(i, e, slot).wait()
        for e in range(N_EXPERTS):
            rows = win_ref[i * N_EXPERTS + e] + lax.broadcasted_iota(jnp.int32, (COMBINE_TM, COMBINE_WIN), 1)
            oh_ref[:, e * COMBINE_WIN:(e + 1) * COMBINE_WIN] = jnp.where(growt[:, e:e + 1] == rows, 1.0, 0.0).astype(BF16)
        acc_ref[...] = jnp.dot(oh_ref[...], wbuf_ref[slot], preferred_element_type=F32)

    @pl.when(fast_ref[i] == 0)
    def _():
        acc_ref[...] = jnp.zeros_like(acc_ref)
        for e in range(N_EXPERTS):
            lo = bounds_ref[(i * N_EXPERTS + e) * 2]
            hi = bounds_ref[(i * N_EXPERTS + e) * 2 + 1]
            for c in range(2):
                @pl.when((lo < (c + 1) * half) & (hi > c * half))
                def _():
                    cp = pltpu.make_async_copy(ye_hbm.at[e, pl.ds((i // tpg) * GROUP_ROWS + c * half, half), :],
                                               sbuf_ref, ssem)
                    cp.start()
                    cp.wait()
                    rows = lax.broadcasted_iota(jnp.int32, (COMBINE_TM, half), 1) + c * half
                    onehot = jnp.where(growt[:, e:e + 1] == rows, 1.0, 0.0).astype(BF16)
                    acc_ref[...] += jnp.dot(onehot, sbuf_ref[...], preferred_element_type=F32)

    y = x1_ref[...] + g2_ref[...] * _rms(acc_ref[...], gpost2_ref[...])

    @pl.when(i < ctx_steps)
    def _():
        oc_ref[...] = y

    @pl.when(i >= ctx_steps)
    def _():
        ol_ref[...] = y


def _combine(growt, ye, x1, mod3, g_post2, ctx_tokens):
    T = x1.shape[0]
    tm = COMBINE_TM
    tpg = GROUP_TOKENS // tm
    g = growt.reshape(T // tm, tm, N_EXPERTS)
    lo = jnp.min(jnp.where(g >= 0, g, GROUP_ROWS), axis=1)
    hi = jnp.max(g, axis=1) + 1
    win = jnp.clip((lo // BF16_ROW_TILE) * BF16_ROW_TILE, 0, GROUP_ROWS - COMBINE_WIN)
    fast = jnp.all((hi <= lo) | (hi <= win + COMBINE_WIN), axis=1)
    bounds = jnp.stack([lo, hi], axis=-1)
    i32 = lambda a: a.reshape(-1).astype(jnp.int32)
    nct = ctx_tokens // tm
    return pl.pallas_call(
        functools.partial(_combine_kernel, ctx_steps=nct),
        grid_spec=pltpu.PrefetchScalarGridSpec(
            num_scalar_prefetch=3,
            grid=(T // tm,),
            in_specs=[
                pl.BlockSpec((tm, N_EXPERTS), lambda i, *_: (i, 0)),
                pl.BlockSpec(memory_space=pl.ANY),
                pl.BlockSpec((tm, D_MODEL), lambda i, *_: (i, 0)),
                pl.BlockSpec((None, 1, D_MODEL), lambda i, *_: (i // tpg, 0, 5)),
                pl.BlockSpec((1, D_MODEL), lambda i, *_: (0, 0)),
            ],
            out_specs=[pl.BlockSpec((tm, D_MODEL), lambda i, *_: (jnp.minimum(i, nct - 1), 0)),
                       pl.BlockSpec((tm, D_MODEL), lambda i, *_: (jnp.maximum(i - nct, 0), 0))],
            scratch_shapes=[
                pltpu.VMEM((2, N_EXPERTS * COMBINE_WIN, D_MODEL), BF16),
                pltpu.VMEM((tm, N_EXPERTS * COMBINE_WIN), BF16),
                pltpu.VMEM((tm, D_MODEL), F32),
                pltpu.VMEM((GROUP_ROWS // 2, D_MODEL), BF16),
                pltpu.SemaphoreType.DMA((2,)),
                pltpu.SemaphoreType.DMA(()),
            ],
        ),
        out_shape=[jax.ShapeDtypeStruct((ctx_tokens, D_MODEL), F32),
                   jax.ShapeDtypeStruct((T - ctx_tokens, D_MODEL), F32)],
        compiler_params=_cparams(("arbitrary",), 48),
        name="combine",
    )(i32(win), i32(fast), i32(bounds), growt, ye, x1, mod3, g_post2)


def _rope_tables(n):
    rows = n // GRID_W
    row = jnp.repeat(jnp.arange(rows), GRID_W).astype(F32)
    col = jnp.tile(jnp.arange(GRID_W), rows).astype(F32)
    inv = ROPE_BASE ** (-jnp.arange(0, HD // 2, 2, dtype=F32) / (HD // 2))
    ang = jnp.concatenate([row[:, None] * inv, col[:, None] * inv], axis=-1)
    cos, sin = jnp.cos(ang), jnp.sin(ang)
    return jnp.tile(cos, (1, 4)), jnp.tile(jnp.concatenate([-sin, sin], axis=-1), (1, 2))


def kernel(x_prompt, x_sample, cache_k, cache_v, state_rnn, c, c_ctx, w_mod, b_mod, g_pre1, g_post1, g_pre2, g_post2, w_in, conv_w, conv_b, lru_wr, lru_br, lru_wi, lru_bi, lru_lam, lam_q1, lam_k1, lam_q2, lam_k2, g_sub, w_a_out, w_b_out, w_o, w_router, w_e_gate, w_e_up, w_e_down):
    nbc, nc, _ = x_prompt.shape
    nbl, nl, _ = x_sample.shape
    assert nbc * nc == GROUP_TOKENS and nl == GROUP_TOKENS and w_mod.shape[0] == 1
    tc = nbc * nc
    x_c, x_l = x_prompt.reshape(tc, D_MODEL), x_sample.reshape(nbl * nl, D_MODEL)
    sq0 = lambda a: a.reshape(a.shape[1:])
    sq1 = lambda a: a.reshape(a.shape[:1] + a.shape[2:])
    (w_mod, w_in, conv_w, lru_wr, lru_wi, lru_br, lru_bi, lru_lam, w_a_out, w_b_out, w_o, w_router, w_e_gate, w_e_up,
     w_e_down) = map(sq0, (w_mod, w_in, conv_w, lru_wr, lru_wi, lru_br, lru_bi, lru_lam, w_a_out, w_b_out, w_o,
                           w_router, w_e_gate, w_e_up, w_e_down))
    state_rnn, cache_k, cache_v = map(sq1, (state_rnn, cache_k, cache_v))

    cvec = jnp.concatenate([c_ctx[None, :], c, jnp.zeros((SUBLANES - 1 - nbl, D_MODEL), F32)], axis=0)
    mod = _modulation(cvec, w_mod, b_mod)
    mod3 = mod.reshape(SUBLANES, 1, 6 * D_MODEL)

    proj = _in_proj(x_c, x_l, mod3, g_pre1, w_in.astype(BF16))

    rnn_w = (conv_w, conv_b, lru_wr, lru_wi, lru_br, lru_bi, lru_lam)
    ya_c, h_fin = _rnn_branch(proj, *rnn_w, jnp.zeros((nbc, 2, D_RNN), F32), seq=nc, nseq=SUBLANES,
                              nb=nbc // SUBLANES, row0=0)
    ya_l, _ = _rnn_branch(proj, *rnn_w, state_rnn, seq=nl, nseq=1, nb=nbl, row0=tc // nl)

    lam_vecs = (lam_q1, lam_k1, lam_q2, lam_k2)
    on_c = _attention(proj, lam_vecs, g_sub, n=nc, nb=nbc, row0=0, tq=nc)
    past = cache_k.shape[1]
    on_l = _attention(proj, lam_vecs, g_sub, n=nl, nb=nbl, row0=tc // nl, tq=512,
                      cache=(cache_k.reshape(nbl, past, N_HEADS * VD), cache_v.reshape(nbl, past, N_HEADS * VD)),
                      tables=_rope_tables(nl))

    x1, h2, aff_t = _post_mix(ya_c, ya_l, on_c, on_l, proj, x_c, x_l, mod3, g_post1, g_pre2, w_a_out.astype(BF16),
                              w_b_out.astype(BF16), w_o.astype(BF16), w_router.T)

    capc = EC_FACTOR * nc // N_EXPERTS
    idx_c, val_c, growt_c = _route(aff_t, n=nc, nsets=nbc, col0=0, row_stride=capc)
    idx_l, val_l, growt_l = _route(aff_t, n=nl, nsets=nbl, col0=tc // nl, row_stride=0)
    gid_c = idx_c[..., 0] + (jnp.arange(nbc, dtype=jnp.int32) * nc)[:, None, None]
    gid_l = idx_l[..., 0] + (tc + jnp.arange(nbl, dtype=jnp.int32) * nl)[:, None, None]
    gidx = jnp.concatenate([gid_c.transpose(1, 0, 2).reshape(N_EXPERTS, -1),
                            gid_l.transpose(1, 0, 2).reshape(N_EXPERTS, -1)], axis=1)
    vals = jnp.concatenate([val_c.transpose(1, 0, 2, 3).reshape(N_EXPERTS, -1, 1),
                            val_l.transpose(1, 0, 2, 3).reshape(N_EXPERTS, -1, 1)], axis=1)
    growt = jnp.concatenate([growt_c, growt_l], axis=0)

    ye = _experts(gidx, h2, vals, w_e_gate, w_e_up, w_e_down)
    y_c, y_l = _combine(growt, ye, x1, mod3, g_post2, tc)

    y_prompt = y_c.reshape(nbc, nc, D_MODEL)
    y_sample = y_l.reshape(nbl, nl, D_MODEL)
    kblk = (2 * D_RNN + N_HEADS * VD) // (N_HEADS * VD)
    new_k = _take_cols(proj, tc, kblk, N_HEADS * VD).reshape(nbc, 1, nc, N_HEADS, 2 * HD)
    new_v = _take_cols(proj, tc, kblk + 1, N_HEADS * VD).reshape(nbc, 1, nc, N_HEADS, VD)
    return (y_prompt, y_sample, new_k, new_v, h_fin[:, None])
```

```python
import functools
import math

import jax
import jax.numpy as jnp
from jax import lax
from jax.experimental import pallas as pl
from jax.experimental.pallas import tpu as pltpu

F32 = jnp.float32
BF16 = jnp.bfloat16
HIGHEST = lax.Precision.HIGHEST

D_MODEL = 2048
D_RNN = 1024
RNN_BLOCKS = 8
RNN_BS = 128
CONV_W = 4
LRU_C = 8.0
N_HEADS = 8
HD = 64
VD = 128
N_EXPERTS = 16
D_EXPERT = 1024
EC_FACTOR = 2
EPS = 1e-6
IN_COLS = 9216
GRID_W = 64
ROPE_BASE = 10000.0
LAM_INIT = 0.8 - 0.6 * math.exp(-0.3 * 0)

GROUP_TOKENS = 4096
GROUP_ROWS = 512
SUBLANES = 8
LANES = 128
V7X_VMEM_BYTES = 64 * 1024 * 1024


def _cparams(semantics, vmem_mb):
    return pltpu.CompilerParams(dimension_semantics=semantics, vmem_limit_bytes=vmem_mb * 1024 * 1024)


def _sigmoid(x):
    return 0.5 * jnp.tanh(0.5 * x) + 0.5


def _rms(x, g):
    return x * lax.rsqrt(jnp.mean(x * x, axis=-1, keepdims=True) + EPS) * g


def _mod_kernel(c_ref, w_ref, b_ref, o_ref):
    c = c_ref[...]
    s = c * _sigmoid(c)
    o_ref[...] = jnp.dot(s, w_ref[...], precision=HIGHEST, preferred_element_type=F32) + b_ref[...]


def _modulation(cvec, w_mod, b_mod):
    tn = 1024
    return pl.pallas_call(
        _mod_kernel,
        grid=(6 * D_MODEL // tn,),
        in_specs=[
            pl.BlockSpec((SUBLANES, D_MODEL), lambda j: (0, 0)),
            pl.BlockSpec((D_MODEL, tn), lambda j: (0, j)),
            pl.BlockSpec((1, tn), lambda j: (0, j)),
        ],
        out_specs=pl.BlockSpec((SUBLANES, tn), lambda j: (0, j)),
        out_shape=jax.ShapeDtypeStruct((SUBLANES, 6 * D_MODEL), F32),
        compiler_params=_cparams(("arbitrary",), 40),
        name="modulation",
    )(cvec, w_mod, b_mod)


def _in_kernel(xc_ref, xl_ref, sh_ref, sc_ref, g_ref, w_ref, o_ref, h_ref, *, ctx_tiles):
    i, j = pl.program_id(0), pl.program_id(1)
    rows = 128

    def prologue(x_ref):
        def body(c, carry):
            r0 = pl.multiple_of(c * rows, rows)
            h = _rms(x_ref[pl.ds(r0, rows), :], g_ref[...]) * (1.0 + sc_ref[...]) + sh_ref[...]
            h_ref[pl.ds(r0, rows), :] = h.astype(BF16)
            return carry

        lax.fori_loop(0, h_ref.shape[0] // rows, body, 0)

    @pl.when((j == 0) & (i < ctx_tiles))
    def _():
        prologue(xc_ref)

    @pl.when((j == 0) & (i >= ctx_tiles))
    def _():
        prologue(xl_ref)

    o_ref[...] = jnp.dot(h_ref[...], w_ref[...], preferred_element_type=F32)


def _in_proj(x_c, x_l, mod3, g_pre1, w_in_bf):
    T = x_c.shape[0] + x_l.shape[0]
    tm, tn = 1024, 1024
    tpg = GROUP_TOKENS // tm
    nct = x_c.shape[0] // tm
    return pl.pallas_call(
        functools.partial(_in_kernel, ctx_tiles=nct),
        grid=(T // tm, IN_COLS // tn),
        in_specs=[
            pl.BlockSpec((tm, D_MODEL), lambda i, j: (jnp.minimum(i, nct - 1), 0), pipeline_mode=pl.Buffered(1)),
            pl.BlockSpec((tm, D_MODEL), lambda i, j: (jnp.maximum(i - nct, 0), 0)),
            pl.BlockSpec((None, 1, D_MODEL), lambda i, j: (i // tpg, 0, 0)),
            pl.BlockSpec((None, 1, D_MODEL), lambda i, j: (i // tpg, 0, 1)),
            pl.BlockSpec((1, D_MODEL), lambda i, j: (0, 0)),
            pl.BlockSpec((D_MODEL, tn), lambda i, j: (0, j)),
        ],
        out_specs=pl.BlockSpec((tm, tn), lambda i, j: (i, j)),
        out_shape=jax.ShapeDtypeStruct((T, IN_COLS), F32),
        scratch_shapes=[pltpu.VMEM((tm, D_MODEL), BF16)],
        compiler_params=_cparams(("arbitrary", "arbitrary"), 56),
        name="in_proj",
    )(x_c, x_l, mod3, mod3, g_pre1, w_in_bf)


def _copy_kernel(x_ref, o_ref):
    o_ref[...] = x_ref[...]


def _take_cols(proj, rows, col_block, width):
    tm = 1024
    return pl.pallas_call(
        _copy_kernel,
        grid=(rows // tm,),
        in_specs=[pl.BlockSpec((tm, width), lambda i: (i, col_block))],
        out_specs=pl.BlockSpec((tm, width), lambda i: (i, 0)),
        out_shape=jax.ShapeDtypeStruct((rows, width), proj.dtype),
        compiler_params=_cparams(("arbitrary",), 40),
        name="take_cols",
    )(proj)


def _gelu_tanh(x):
    return 0.5 * x * (1.0 + jnp.tanh(math.sqrt(2.0 / math.pi) * (x + 0.044715 * (x * x * x))))


def _softplus(x):
    return jnp.maximum(x, 0.0) + jnp.log(1.0 + jnp.exp(-jnp.abs(x)))


def _split_bf16(x):
    hi = x.astype(BF16)
    return hi, (x - hi.astype(F32)).astype(BF16)


def _dot_3pass(x, w_hi, w_lo):
    x_hi, x_lo = _split_bf16(x)
    d = functools.partial(jnp.dot, preferred_element_type=F32)
    return d(x_hi, w_hi) + (d(x_lo, w_hi) + d(x_hi, w_lo))


def _rnn_kernel(xr_ref, gr_ref, cw_ref, cb_ref, wr_ref, wi_ref, br_ref, bi_ref, lam_ref, h0_ref,
                ya_ref, hf_ref, xp_ref, a_ref, b_ref, h_ref, p_ref, *, seq, nseq):
    n = seq * nseq
    seg = n // SUBLANES
    pitch = seg + SUBLANES
    zeros8 = jnp.zeros((SUBLANES, LANES), F32)
    xp_ref[0:SUBLANES, :] = zeros8
    xp_ref[SUBLANES + n:2 * SUBLANES + n, :] = zeros8
    xp_ref[SUBLANES:SUBLANES + n, :] = xr_ref[...]
    cw = cw_ref[...]
    cb = cb_ref[...]
    sp = _softplus(-lam_ref[...])
    w_r = [_split_bf16(wr_ref[d]) for d in range(2)]
    w_i = [_split_bf16(wi_ref[d]) for d in range(2)]

    def gates(s, carry):
        r0 = pl.multiple_of(s * seg, SUBLANES)
        taps = [xp_ref[pl.ds(r0 + 6 + k, seg), :] for k in range(CONV_W)]
        if nseq > 1:
            t = (r0 + lax.broadcasted_iota(jnp.int32, (seg, LANES), 0)) % seq
            taps[0] = jnp.where(t >= 2, taps[0], 0.0)
            taps[1] = jnp.where(t >= 1, taps[1], 0.0)
            taps[3] = jnp.where(t <= seq - 2, taps[3], 0.0)
        xc = cb + taps[0] * cw[0:1]
        xc = xc + taps[1] * cw[1:2]
        xc = xc + taps[2] * cw[2:3]
        xc = xc + taps[3] * cw[3:4]
        p0 = pl.multiple_of(s * pitch, SUBLANES)
        for d in range(2):
            r = _sigmoid(_dot_3pass(xc, *w_r[d]) + br_ref[d:d + 1, :])
            i = _sigmoid(_dot_3pass(xc, *w_i[d]) + bi_ref[d:d + 1, :])
            a = jnp.exp((-LRU_C) * r * sp[d:d + 1, :])
            y = 1.0 - a * a
            a_ref[d, pl.ds(p0, seg), :] = a
            b_ref[d, pl.ds(p0, seg), :] = jnp.where(y > 0.0, y * lax.rsqrt(y), 0.0) * (i * xc)
        return carry

    lax.fori_loop(0, SUBLANES, gates, 0)

    def scan(j, carry):
        hf, af, hb, ab = carry
        a0 = a_ref[0, pl.ds(j, SUBLANES, stride=pitch), :]
        b0 = b_ref[0, pl.ds(j, SUBLANES, stride=pitch), :]
        hf = a0 * hf + b0
        af = a0 * af
        h_ref[0, pl.ds(j, SUBLANES, stride=pitch), :] = hf
        p_ref[0, pl.ds(j, SUBLANES, stride=pitch), :] = af
        jb = seg - 1 - j
        a1 = a_ref[1, pl.ds(jb, SUBLANES, stride=pitch), :]
        b1 = b_ref[1, pl.ds(jb, SUBLANES, stride=pitch), :]
        hb = a1 * hb + b1
        ab = a1 * ab
        h_ref[1, pl.ds(jb, SUBLANES, stride=pitch), :] = hb
        p_ref[1, pl.ds(jb, SUBLANES, stride=pitch), :] = ab
        return hf, af, hb, ab

    ones8 = jnp.ones((SUBLANES, LANES), F32)
    ef, pf, eb, pb = lax.fori_loop(0, seg, scan, (zeros8, ones8, zeros8, ones8), unroll=8)

    spq = SUBLANES // nseq
    cf = [None] * SUBLANES
    cbk = [None] * SUBLANES
    for q in range(nseq):
        h0 = h0_ref[q]
        first, last = q * spq, q * spq + spq - 1
        cf[first] = h0[0:1, :]
        for s in range(first + 1, last + 1):
            cf[s] = ef[s - 1:s, :] + pf[s - 1:s, :] * cf[s - 1]
        cbk[last] = h0[1:2, :]
        for s in range(last - 1, first - 1, -1):
            cbk[s] = eb[s + 1:s + 2, :] + pb[s + 1:s + 2, :] * cbk[s + 1]
        hf_ref[q, 0:1, :] = ef[last:last + 1, :] + pf[last:last + 1, :] * cf[last]
        hf_ref[q, 1:2, :] = eb[first:first + 1, :] + pb[first:first + 1, :] * cbk[first]

    for s in range(SUBLANES):
        hfw = h_ref[0, s * pitch:s * pitch + seg, :] + p_ref[0, s * pitch:s * pitch + seg, :] * cf[s]
        hbw = h_ref[1, s * pitch:s * pitch + seg, :] + p_ref[1, s * pitch:s * pitch + seg, :] * cbk[s]
        y = (hfw + hbw) * _gelu_tanh(gr_ref[s * seg:(s + 1) * seg, :])
        ya_ref[s * seg:(s + 1) * seg, :] = y.astype(ya_ref.dtype)


def _rnn_branch(proj, conv_w, conv_b, wr, wi, br, bi, lam, h0, *, seq, nseq, nb, row0):
    n = seq * nseq
    seg = n // SUBLANES
    pitch = seg + SUBLANES
    return pl.pallas_call(
        functools.partial(_rnn_kernel, seq=seq, nseq=nseq),
        grid=(nb, RNN_BLOCKS),
        in_specs=[
            pl.BlockSpec((n, RNN_BS), lambda b, k: (row0 + b, k)),
            pl.BlockSpec((n, RNN_BS), lambda b, k: (row0 + b, RNN_BLOCKS + k)),
            pl.BlockSpec((CONV_W, RNN_BS), lambda b, k: (0, k)),
            pl.BlockSpec((1, RNN_BS), lambda b, k: (0, k)),
            pl.BlockSpec((2, None, RNN_BS, RNN_BS), lambda b, k: (0, k, 0, 0)),
            pl.BlockSpec((2, None, RNN_BS, RNN_BS), lambda b, k: (0, k, 0, 0)),
            pl.BlockSpec((2, RNN_BS), lambda b, k: (0, k)),
            pl.BlockSpec((2, RNN_BS), lambda b, k: (0, k)),
            pl.BlockSpec((2, RNN_BS), lambda b, k: (0, k)),
            pl.BlockSpec((nseq, 2, RNN_BS), lambda b, k: (b, 0, k)),
        ],
        out_specs=[
            pl.BlockSpec((n, RNN_BS), lambda b, k: (b, k)),
            pl.BlockSpec((nseq, 2, RNN_BS), lambda b, k: (b, 0, k)),
        ],
        out_shape=[jax.ShapeDtypeStruct((nb * n, D_RNN), BF16), jax.ShapeDtypeStruct((nb * nseq, 2, D_RNN), F32)],
        scratch_shapes=[
            pltpu.VMEM((n + 2 * SUBLANES, RNN_BS), F32),
        ] + [pltpu.VMEM((2, SUBLANES * pitch, RNN_BS), F32)] * 4,
        compiler_params=_cparams(("arbitrary", "arbitrary"), 40),
        name=f"rnn_n{n}",
    )(proj, proj, conv_w, conv_b, wr, wi, br, bi, lam, h0)


def _rope(x, cos, sin_signed):
    lane = lax.broadcasted_iota(jnp.int32, x.shape, 1)
    partner = jnp.where((lane % HD) < HD // 2, pltpu.roll(x, LANES - HD // 2, 1), pltpu.roll(x, HD // 2, 1))
    return x * cos + partner * sin_signed


ATTN_SUB = 128
ATTN_LOOKAHEAD = 2


def _attn_kernel(*refs, n, npast, tq, rope):
    if rope:
        (lq1, lk1, lq2, lk2, q_ref, k_ref, v_ref, gsub_ref, ck_ref, cv_ref, cosk_ref, sink_ref, cosq_ref, sinq_ref,
         o_ref, kb_ref, vb_ref, s_ref, p_ref) = refs
    else:
        lq1, lk1, lq2, lk2, q_ref, k_ref, v_ref, gsub_ref, o_ref, kb_ref, vb_ref, s_ref, p_ref = refs
    chunk = 256
    nk = n + npast

    @pl.when(pl.program_id(2) == 0)
    def _():
        ones_col = jnp.where(lax.broadcasted_iota(jnp.int32, (chunk, VD), 1) == 0, 1.0, 0.0).astype(BF16)
        if npast:
            for r0 in range(0, npast, chunk):
                kb_ref[:, r0:r0 + chunk] = ck_ref[r0:r0 + chunk, :].T.astype(BF16)
                vb_ref[r0:r0 + chunk, VD:2 * VD] = ones_col
            vb_ref[0:npast, 0:VD] = cv_ref[...].astype(BF16)

        for r0 in range(0, n, chunk):
            k = k_ref[r0:r0 + chunk, :]
            if rope:
                k = _rope(k, cosk_ref[r0:r0 + chunk, :], sink_ref[r0:r0 + chunk, :])
            kb_ref[:, npast + r0:npast + r0 + chunk] = k.T.astype(BF16)
            vb_ref[npast + r0:npast + r0 + chunk, 0:VD] = v_ref[r0:r0 + chunk, :].astype(BF16)
            vb_ref[npast + r0:npast + r0 + chunk, VD:2 * VD] = ones_col

    lam = (jnp.exp(jnp.sum(lq1[...] * lk1[...], axis=-1, keepdims=True))
           - jnp.exp(jnp.sum(lq2[...] * lk2[...], axis=-1, keepdims=True)) + LAM_INIT)
    q = q_ref[...]
    if rope:
        q = _rope(q, cosq_ref[...], sinq_ref[...])
    q = q * (HD ** -0.5 * math.log2(math.e))
    lane = lax.broadcasted_iota(jnp.int32, (ATTN_SUB, VD), 1)
    nt = (((1,), (1,)), ((), ()))
    kc = 2 * LANES
    key_chunks = [(c0, min(kc, nk - c0)) for c0 in range(0, nk, kc)]
    nunits = 2 * (tq // ATTN_SUB)

    def scores(unit):
        sub, comp = divmod(unit, 2)
        qs = q[sub * ATTN_SUB:(sub + 1) * ATTN_SUB, :]
        qc = jnp.where((lane < HD) == (comp == 0), qs, 0.0).astype(BF16)
        m_run = None
        for c0, w in key_chunks:
            s = jnp.dot(qc, kb_ref[:, c0:c0 + w], preferred_element_type=F32)
            s_ref[unit, :, c0:c0 + w] = s
            for l0 in range(0, w, LANES):
                piece = s[:, l0:l0 + LANES]
                m_run = piece if m_run is None else jnp.maximum(m_run, piece)
        return jnp.max(m_run, axis=-1, keepdims=True)

    def values(unit, m):
        for c0, w in key_chunks:
            p_ref[unit, :, c0:c0 + w] = jnp.exp2(s_ref[unit, :, c0:c0 + w] - m).astype(BF16)
        return jnp.dot(p_ref[unit], vb_ref[...], preferred_element_type=F32)

    outs = [None] * nunits
    ahead = ATTN_LOOKAHEAD
    ms = [scores(u) for u in range(min(ahead, nunits))]
    for unit in range(nunits):
        if unit + ahead < nunits:
            ms.append(scores(unit + ahead))
        outs[unit] = values(unit, ms[unit])
        if unit % 2 == 1:
            sub = unit // 2
            o1, o2 = outs[unit - 1], outs[unit]
            o = o1[:, 0:VD] * (1.0 / o1[:, VD:VD + 1]) - o2[:, 0:VD] * (lam / o2[:, VD:VD + 1])
            o = _rms(o, gsub_ref[...]) * (1.0 - LAM_INIT)
            o_ref[sub * ATTN_SUB:(sub + 1) * ATTN_SUB, :] = o.astype(o_ref.dtype)


def _attention(proj, lam_vecs, g_sub, *, n, nb, row0, tq, cache=None, tables=None):
    rope = tables is not None
    npast = 0 if cache is None else cache[0].shape[1]
    nk = n + npast
    qpr = n // tq
    qcol, kcol, vcol = 2 * RNN_BLOCKS, 2 * RNN_BLOCKS + N_HEADS, 2 * RNN_BLOCKS + 2 * N_HEADS
    vec = pl.BlockSpec((1, HD), lambda b, h, t: (0, 0))
    in_specs = [vec, vec, vec, vec,
                pl.BlockSpec((tq, VD), lambda b, h, t: ((row0 + b) * qpr + t, qcol + h)),
                pl.BlockSpec((n, VD), lambda b, h, t: (row0 + b, kcol + h)),
                pl.BlockSpec((n, VD), lambda b, h, t: (row0 + b, vcol + h)),
                pl.BlockSpec((1, VD), lambda b, h, t: (0, 0))]
    args = [*lam_vecs, proj, proj, proj, g_sub]
    if rope:
        ck, cv = cache
        cos_t, sin_t = tables
        in_specs += [pl.BlockSpec((None, npast, VD), lambda b, h, t: (b, 0, h)),
                     pl.BlockSpec((None, npast, VD), lambda b, h, t: (b, 0, h)),
                     pl.BlockSpec((n, VD), lambda b, h, t: (0, 0), pipeline_mode=pl.Buffered(1)),
                     pl.BlockSpec((n, VD), lambda b, h, t: (0, 0), pipeline_mode=pl.Buffered(1)),
                     pl.BlockSpec((tq, VD), lambda b, h, t: (t, 0)),
                     pl.BlockSpec((tq, VD), lambda b, h, t: (t, 0))]
        args += [ck, cv, cos_t, sin_t, cos_t, sin_t]
    return pl.pallas_call(
        functools.partial(_attn_kernel, n=n, npast=npast, tq=tq, rope=rope),
        grid=(nb, N_HEADS, qpr),
        in_specs=in_specs,
        out_specs=pl.BlockSpec((tq, VD), lambda b, h, t: (b * qpr + t, h)),
        out_shape=jax.ShapeDtypeStruct((nb * n, N_HEADS * VD), BF16),
        scratch_shapes=[pltpu.VMEM((VD, nk), BF16), pltpu.VMEM((nk, 2 * VD), BF16),
                        pltpu.VMEM((2 * tq // ATTN_SUB, ATTN_SUB, nk), F32),
                        pltpu.VMEM((2 * tq // ATTN_SUB, ATTN_SUB, nk), BF16)],
        compiler_params=_cparams(("arbitrary", "arbitrary", "arbitrary"), 56),
        name=f"attn_n{n}",
    )(*args)


POST_TM = 512
POST_UNIT = 256


def _merge_kernel(yac_ref, yal_ref, onc_ref, onl_ref, ga0_ref, ga1_ref, gb0_ref, gb1_ref, wa_ref, wb_ref, m_ref,
                  *, ctx_tiles):
    is_ctx = pl.program_id(0) < ctx_tiles
    half = D_MODEL // 2
    for r0 in range(0, POST_TM, POST_UNIT):
        rs = slice(r0, r0 + POST_UNIT)
        ya = jnp.where(is_ctx, yac_ref[rs, :], yal_ref[rs, :])
        on = jnp.where(is_ctx, onc_ref[rs, :], onl_ref[rs, :])
        oa = jnp.dot(ya, wa_ref[...], preferred_element_type=F32)
        ob = jnp.dot(on, wb_ref[...], preferred_element_type=F32)
        m_ref[rs, 0:half] = (_sigmoid(ga0_ref[rs, :]) * oa[:, 0:half]
                             + _sigmoid(gb0_ref[rs, :]) * ob[:, 0:half]).astype(BF16)
        m_ref[rs, half:] = (_sigmoid(ga1_ref[rs, :]) * oa[:, half:]
                            + _sigmoid(gb1_ref[rs, :]) * ob[:, half:]).astype(BF16)


def _outproj_kernel(m_ref, xc_ref, xl_ref, g1_ref, sh2_ref, sc2_ref, gpost1_ref, gpre2_ref, wo_ref, wrt_ref,
                    x1_ref, h2_ref, aff_ref, *, ctx_tiles):
    is_ctx = pl.program_id(0) < ctx_tiles
    nt = functools.partial(lax.dot_general, dimension_numbers=(((1,), (1,)), ((), ())), preferred_element_type=F32)
    w_hi, w_lo = _split_bf16(wrt_ref[...])
    units = [slice(r0, r0 + POST_UNIT) for r0 in range(0, POST_TM, POST_UNIT)]
    ts = [jnp.dot(m_ref[rs, :], wo_ref[...], preferred_element_type=F32) for rs in units]
    for rs, t in zip(units, ts):
        x1 = jnp.where(is_ctx, xc_ref[rs, :], xl_ref[rs, :]) + g1_ref[...] * _rms(t, gpost1_ref[...])
        x1_ref[rs, :] = x1
        h2 = _rms(x1, gpre2_ref[...]) * (1.0 + sc2_ref[...]) + sh2_ref[...]
        h2_ref[rs, :] = h2
        h_hi, h_lo = _split_bf16(h2)
        logits = nt(w_hi, h_hi) + (nt(w_lo, h_hi) + nt(w_hi, h_lo))
        e = jnp.exp(logits - jnp.max(logits, axis=0, keepdims=True))
        aff_ref[:, rs] = e / jnp.sum(e, axis=0, keepdims=True)


def _post_mix(ya_c, ya_l, on_c, on_l, proj, x_c, x_l, mod3, g_post1, g_pre2, wa, wb, wo, wrt):
    T = x_c.shape[0] + x_l.shape[0]
    tm = POST_TM
    tpg = GROUP_TOKENS // tm
    nct = ya_c.shape[0] // tm
    ctx_blk = lambda w: pl.BlockSpec((tm, w), lambda i: (jnp.minimum(i, nct - 1), 0), pipeline_mode=pl.Buffered(1))
    lat_blk = lambda w: pl.BlockSpec((tm, w), lambda i: (jnp.maximum(i - nct, 0), 0))
    half = D_MODEL // 2
    gate0 = (2 * D_RNN + 2 * N_HEADS * VD + N_HEADS * VD) // half
    row = lambda c: pl.BlockSpec((None, 1, D_MODEL), lambda i: (i // tpg, 0, c))
    const = lambda shape: pl.BlockSpec(shape, lambda i: (0,) * len(shape), pipeline_mode=pl.Buffered(1))
    merged = pl.pallas_call(
        functools.partial(_merge_kernel, ctx_tiles=nct),
        grid=(T // tm,),
        in_specs=[
            ctx_blk(D_RNN), lat_blk(D_RNN), ctx_blk(N_HEADS * VD), lat_blk(N_HEADS * VD),
            pl.BlockSpec((tm, half), lambda i: (i, gate0)),
            pl.BlockSpec((tm, half), lambda i: (i, gate0 + 1)),
            pl.BlockSpec((tm, half), lambda i: (i, gate0 + 2)),
            pl.BlockSpec((tm, half), lambda i: (i, gate0 + 3)),
            const((D_RNN, D_MODEL)), const((N_HEADS * VD, D_MODEL)),
        ],
        out_specs=pl.BlockSpec((tm, D_MODEL), lambda i: (i, 0)),
        out_shape=jax.ShapeDtypeStruct((T, D_MODEL), BF16),
        compiler_params=_cparams(("arbitrary",), 56),
        name="merge",
    )(ya_c, ya_l, on_c, on_l, proj, proj, proj, proj, wa, wb)
    return pl.pallas_call(
        functools.partial(_outproj_kernel, ctx_tiles=nct),
        grid=(T // tm,),
        in_specs=[
            pl.BlockSpec((tm, D_MODEL), lambda i: (i, 0)),
            ctx_blk(D_MODEL), lat_blk(D_MODEL),
            row(2), row(3), row(4),
            const((1, D_MODEL)), const((1, D_MODEL)),
            const((D_MODEL, D_MODEL)),
            const((N_EXPERTS, D_MODEL)),
        ],
        out_specs=[
            pl.BlockSpec((tm, D_MODEL), lambda i: (i, 0)),
            pl.BlockSpec((tm, D_MODEL), lambda i: (i, 0)),
            pl.BlockSpec((N_EXPERTS, tm), lambda i: (0, i)),
        ],
        out_shape=[jax.ShapeDtypeStruct((T, D_MODEL), F32), jax.ShapeDtypeStruct((T, D_MODEL), F32),
                   jax.ShapeDtypeStruct((N_EXPERTS, T), F32)],
        compiler_params=_cparams(("arbitrary",), 56),
        name="out_proj",
    )(merged, x_c, x_l, mod3, mod3, mod3, g_post1, g_pre2, wo, wrt)


def _cumsum_excl(x, tri):
    n = x.shape[1]
    blk = tri.shape[0]
    outs = []
    carry = jnp.zeros((x.shape[0], 1), F32)
    for c in range(n // blk):
        xc = x[:, c * blk:(c + 1) * blk]
        outs.append(jnp.dot(xc.astype(BF16), tri, preferred_element_type=F32) + carry)
        carry = carry + jnp.sum(xc, axis=1, keepdims=True)
    return outs[0] if len(outs) == 1 else jnp.concatenate(outs, axis=1)


def _route_kernel(aff_ref, idx_ref, val_ref, growt_ref, *, n, cap, row_stride):
    aff = aff_ref[...]

    def bisect(_, lohi):
        lo, hi = lohi
        mid = lo + ((hi - lo) >> 1)
        cnt = jnp.sum((aff >= pltpu.bitcast(mid, F32)).astype(jnp.int32), axis=1, keepdims=True)
        ok = cnt >= cap
        return jnp.where(ok, mid, lo), jnp.where(ok, hi, mid)

    lo0 = jnp.zeros((N_EXPERTS, 1), jnp.int32)
    hi0 = jnp.full((N_EXPERTS, 1), 0x7F800000, jnp.int32)
    thr, _ = lax.fori_loop(0, 31, bisect, (lo0, hi0))
    gt = aff >= pltpu.bitcast(thr + 1, F32)
    eq = (aff >= pltpu.bitcast(thr, F32)) & jnp.logical_not(gt)
    need = cap - jnp.sum(gt.astype(jnp.int32), axis=1, keepdims=True)
    blk = min(n, 256)
    tri = (lax.broadcasted_iota(jnp.int32, (blk, blk), 0) < lax.broadcasted_iota(jnp.int32, (blk, blk), 1)).astype(BF16)
    eqpos = _cumsum_excl(eq.astype(F32), tri)
    sel = gt | (eq & (eqpos < need.astype(F32)))
    pos = _cumsum_excl(sel.astype(F32), tri).astype(jnp.int32)
    slot = jnp.where(sel, pos, -1)
    grow = jnp.where(sel, pos + pl.program_id(0) * row_stride, -1)
    growt_ref[...] = grow.astype(F32).T.astype(jnp.int32)

    rc = min(cap, 32)
    tok = lax.broadcasted_iota(jnp.int32, (rc, n), 1)
    for e in range(N_EXPERTS):
        slot_e = slot[e:e + 1, :]
        aff_e = aff[e:e + 1, :]

        def extract(c, carry):
            r0 = pl.multiple_of(c * rc, rc)
            hit = slot_e == (lax.broadcasted_iota(jnp.int32, (rc, n), 0) + r0)
            idx_ref[e, pl.ds(r0, rc), :] = jnp.sum(jnp.where(hit, tok, 0), axis=1, keepdims=True)
            val_ref[e, pl.ds(r0, rc), :] = jnp.sum(jnp.where(hit, aff_e, 0.0), axis=1, keepdims=True)
            return carry

        lax.fori_loop(0, cap // rc, extract, 0)


def _route(aff_t, *, n, nsets, col0, row_stride):
    cap = EC_FACTOR * n // N_EXPERTS
    kern = functools.partial(_route_kernel, n=n, cap=cap, row_stride=row_stride)
    return pl.pallas_call(
        kern,
        grid=(nsets,),
        in_specs=[pl.BlockSpec((N_EXPERTS, n), lambda s: (0, col0 + s))],
        out_specs=[
            pl.BlockSpec((None, N_EXPERTS, cap, 1), lambda s: (s, 0, 0, 0)),
            pl.BlockSpec((None, N_EXPERTS, cap, 1), lambda s: (s, 0, 0, 0)),
            pl.BlockSpec((n, N_EXPERTS), lambda s: (s, 0)),
        ],
        out_shape=[jax.ShapeDtypeStruct((nsets, N_EXPERTS, cap, 1), jnp.int32),
                   jax.ShapeDtypeStruct((nsets, N_EXPERTS, cap, 1), F32),
                   jax.ShapeDtypeStruct((nsets * n, N_EXPERTS), jnp.int32)],
        compiler_params=_cparams(("arbitrary",), 48),
        name=f"route_n{n}",
    )(aff_t)


WEIGHT_PARTS = 4


def _expert_kernel(cur_ref, nxt_ref, h2_hbm, val_ref, wg_hbm, wu_hbm, wd_hbm, ye_ref, xs0_ref, xs1_ref,
                   wg_ref, wu_ref, wd_ref, sg_ref, su_ref, sd_ref, sem, wsem, *, rows):
    e, grp = pl.program_id(0), pl.program_id(1)
    step = e * pl.num_programs(1) + grp
    nsteps = pl.num_programs(0) * pl.num_programs(1)
    bufs = (xs0_ref, xs1_ref)
    rin, rdn = D_MODEL // WEIGHT_PARTS, D_EXPERT // WEIGHT_PARTS

    def slab_copies(ex, part):
        r_in = pl.multiple_of(part * rin, rin)
        r_dn = pl.multiple_of(part * rdn, rdn)
        return (pltpu.make_async_copy(wg_hbm.at[ex, pl.ds(r_in, rin), :], sg_ref, wsem.at[0]),
                pltpu.make_async_copy(wu_hbm.at[ex, pl.ds(r_in, rin), :], su_ref, wsem.at[1]),
                pltpu.make_async_copy(wd_hbm.at[ex, pl.ds(r_dn, rdn), :], sd_ref, wsem.at[2]))

    def fetch(ex, part):
        for cp in slab_copies(ex, part):
            cp.start()

    def land(ex, part):
        for cp in slab_copies(ex, part):
            cp.wait()
        st = ex % 2
        wg_ref[st, pl.ds(pl.multiple_of(part * rin, rin), rin), :] = sg_ref[...].astype(BF16)
        wu_ref[st, pl.ds(pl.multiple_of(part * rin, rin), rin), :] = su_ref[...].astype(BF16)
        wd_ref[st, pl.ds(pl.multiple_of(part * rdn, rdn), rdn), :] = sd_ref[...].astype(BF16)

    @pl.when(step == 0)
    def _():
        for part in range(WEIGHT_PARTS):
            fetch(0, part)
            land(0, part)

    has_next = e + 1 < pl.num_programs(0)

    @pl.when(has_next & (grp >= 1) & (grp <= WEIGHT_PARTS))
    def _():
        land(e + 1, grp - 1)

    @pl.when(has_next & (grp < WEIGHT_PARTS))
    def _():
        fetch(e + 1, grp)

    wset = e % 2

    def row_copy(idx_ref, r, s):
        return pltpu.make_async_copy(h2_hbm.at[pl.ds(idx_ref[0, r], 1), :], bufs[s].at[pl.ds(r, 1), :], sem.at[s])

    def slot_wait(s):
        pltpu.make_async_copy(h2_hbm.at[pl.ds(0, rows), :], bufs[s], sem.at[s]).wait()

    @pl.when(step == 0)
    def _():
        def issue(r, carry):
            row_copy(cur_ref, r, 0).start()
            return carry

        lax.fori_loop(0, rows, issue, 0, unroll=8)

    def body(s):
        slot_wait(s)
        for r in range(rows):
            row_copy(nxt_ref, r, 1 - s).start()
        half = rows // 2
        hids = []
        for c in range(2):
            xs = bufs[s][c * half:(c + 1) * half, :].astype(BF16)
            g = jnp.dot(xs, wg_ref[wset], preferred_element_type=F32)
            u = jnp.dot(xs, wu_ref[wset], preferred_element_type=F32)
            hids.append(((g * _sigmoid(g)) * u).astype(BF16))
        for c in range(2):
            y = jnp.dot(hids[c], wd_ref[wset], preferred_element_type=F32) * val_ref[c * half:(c + 1) * half, :]
            ye_ref[c * half:(c + 1) * half, :] = y.astype(ye_ref.dtype)

        @pl.when(step == nsteps - 1)
        def _():
            slot_wait(1 - s)

    for s in range(2):
        pl.when(step % 2 == s)(functools.partial(body, s))


def _experts(gidx, h2, vals, wg, wu, wd):
    rows_per_expert = vals.shape[1]
    rows = GROUP_ROWS
    groups = rows_per_expert // rows
    assert groups > WEIGHT_PARTS
    last = N_EXPERTS * groups - 1
    hbm = pl.BlockSpec(memory_space=pl.ANY)
    return pl.pallas_call(
        functools.partial(_expert_kernel, rows=rows),
        grid=(N_EXPERTS, groups),
        in_specs=[
            pl.BlockSpec((None, 1, rows), lambda e, g: (e * groups + g, 0, 0), memory_space=pltpu.SMEM),
            pl.BlockSpec((None, 1, rows), lambda e, g: (jnp.minimum(e * groups + g + 1, last), 0, 0),
                         memory_space=pltpu.SMEM),
            hbm,
            pl.BlockSpec((None, rows, 1), lambda e, g: (e, g, 0)),
            hbm, hbm, hbm,
        ],
        out_specs=pl.BlockSpec((None, rows, D_MODEL), lambda e, g: (e, g, 0)),
        out_shape=jax.ShapeDtypeStruct((N_EXPERTS, rows_per_expert, D_MODEL), BF16),
        scratch_shapes=[pltpu.VMEM((rows, D_MODEL), F32), pltpu.VMEM((rows, D_MODEL), F32),
                        pltpu.VMEM((2, D_MODEL, D_EXPERT), BF16), pltpu.VMEM((2, D_MODEL, D_EXPERT), BF16),
                        pltpu.VMEM((2, D_EXPERT, D_MODEL), BF16),
                        pltpu.VMEM((D_MODEL // WEIGHT_PARTS, D_EXPERT), F32),
                        pltpu.VMEM((D_MODEL // WEIGHT_PARTS, D_EXPERT), F32),
                        pltpu.VMEM((D_EXPERT // WEIGHT_PARTS, D_MODEL), F32),
                        pltpu.SemaphoreType.DMA((2,)), pltpu.SemaphoreType.DMA((3,))],
        compiler_params=_cparams(("arbitrary", "arbitrary"), 56),
        name="experts",
    )(gidx.reshape(N_EXPERTS * groups, 1, rows), gidx.reshape(N_EXPERTS * groups, 1, rows), h2, vals, wg, wu, wd)


COMBINE_TM = 256
COMBINE_WIN = 128
BF16_ROW_TILE = 16


def _combine_kernel(win_ref, fast_ref, bounds_ref, growt_ref, ye_hbm, x1_ref, g2_ref, gpost2_ref, oc_ref, ol_ref,
                    wbuf_ref, oh_ref, acc_ref, sbuf_ref, wsem, ssem, *, ctx_steps):
    i = pl.program_id(0)
    nsteps = pl.num_programs(0)
    tpg = GROUP_TOKENS // COMBINE_TM
    slot = i % 2
    half = GROUP_ROWS // 2
    growt = growt_ref[...]

    def window_copy(step, e, s):
        start = pl.multiple_of((step // tpg) * GROUP_ROWS + win_ref[step * N_EXPERTS + e], BF16_ROW_TILE)
        return pltpu.make_async_copy(ye_hbm.at[e, pl.ds(start, COMBINE_WIN), :],
                                     wbuf_ref.at[s, pl.ds(e * COMBINE_WIN, COMBINE_WIN), :], wsem.at[s])

    def fetch(step, s):
        for e in range(N_EXPERTS):
            window_copy(step, e, s).start()

    @pl.when((i == 0) & (fast_ref[0] == 1))
    def _():
        fetch(0, 0)

    nxt = jnp.minimum(i + 1, nsteps - 1)

    @pl.when((i + 1 < nsteps) & (fast_ref[nxt] == 1))
    def _():
        fetch(nxt, 1 - slot)

    @pl.when(fast_ref[i] == 1)
    def _():
        for e in range(N_EXPERTS):
            window_copy(i, e, slot).wait()
        for e in range(N_EXPERTS):
            rows = win_ref[i * N_EXPERTS + e] + lax.broadcasted_iota(jnp.int32, (COMBINE_TM, COMBINE_WIN), 1)
            oh_ref[:, e * COMBINE_WIN:(e + 1) * COMBINE_WIN] = jnp.where(growt[:, e:e + 1] == rows, 1.0, 0.0).astype(BF16)
        acc_ref[...] = jnp.dot(oh_ref[...], wbuf_ref[slot], preferred_element_type=F32)

    @pl.when(fast_ref[i] == 0)
    def _():
        acc_ref[...] = jnp.zeros_like(acc_ref)
        for e in range(N_EXPERTS):
            lo = bounds_ref[(i * N_EXPERTS + e) * 2]
            hi = bounds_ref[(i * N_EXPERTS + e) * 2 + 1]
            for c in range(2):
                @pl.when((lo < (c + 1) * half) & (hi > c * half))
                def _():
                    cp = pltpu.make_async_copy(ye_hbm.at[e, pl.ds((i // tpg) * GROUP_ROWS + c * half, half), :],
                                               sbuf_ref, ssem)
                    cp.start()
                    cp.wait()
                    rows = lax.broadcasted_iota(jnp.int32, (COMBINE_TM, half), 1) + c * half
                    onehot = jnp.where(growt[:, e:e + 1] == rows, 1.0, 0.0).astype(BF16)
                    acc_ref[...] += jnp.dot(onehot, sbuf_ref[...], preferred_element_type=F32)

    y = x1_ref[...] + g2_ref[...] * _rms(acc_ref[...], gpost2_ref[...])

    @pl.when(i < ctx_steps)
    def _():
        oc_ref[...] = y

    @pl.when(i >= ctx_steps)
    def _():
        ol_ref[...] = y


def _combine(growt, ye, x1, mod3, g_post2, ctx_tokens):
    T = x1.shape[0]
    tm = COMBINE_TM
    tpg = GROUP_TOKENS // tm
    g = growt.reshape(T // tm, tm, N_EXPERTS)
    lo = jnp.min(jnp.where(g >= 0, g, GROUP_ROWS), axis=1)
    hi = jnp.max(g, axis=1) + 1
    win = jnp.clip((lo // BF16_ROW_TILE) * BF16_ROW_TILE, 0, GROUP_ROWS - COMBINE_WIN)
    fast = jnp.all((hi <= lo) | (hi <= win + COMBINE_WIN), axis=1)
    bounds = jnp.stack([lo, hi], axis=-1)
    i32 = lambda a: a.reshape(-1).astype(jnp.int32)
    nct = ctx_tokens // tm
    return pl.pallas_call(
        functools.partial(_combine_kernel, ctx_steps=nct),
        grid_spec=pltpu.PrefetchScalarGridSpec(
            num_scalar_prefetch=3,
            grid=(T // tm,),
            in_specs=[
                pl.BlockSpec((tm, N_EXPERTS), lambda i, *_: (i, 0)),
                pl.BlockSpec(memory_space=pl.ANY),
                pl.BlockSpec((tm, D_MODEL), lambda i, *_: (i, 0)),
                pl.BlockSpec((None, 1, D_MODEL), lambda i, *_: (i // tpg, 0, 5)),
                pl.BlockSpec((1, D_MODEL), lambda i, *_: (0, 0)),
            ],
            out_specs=[pl.BlockSpec((tm, D_MODEL), lambda i, *_: (jnp.minimum(i, nct - 1), 0)),
                       pl.BlockSpec((tm, D_MODEL), lambda i, *_: (jnp.maximum(i - nct, 0), 0))],
            scratch_shapes=[
                pltpu.VMEM((2, N_EXPERTS * COMBINE_WIN, D_MODEL), BF16),
                pltpu.VMEM((tm, N_EXPERTS * COMBINE_WIN), BF16),
                pltpu.VMEM((tm, D_MODEL), F32),
                pltpu.VMEM((GROUP_ROWS // 2, D_MODEL), BF16),
                pltpu.SemaphoreType.DMA((2,)),
                pltpu.SemaphoreType.DMA(()),
            ],
        ),
        out_shape=[jax.ShapeDtypeStruct((ctx_tokens, D_MODEL), F32),
                   jax.ShapeDtypeStruct((T - ctx_tokens, D_MODEL), F32)],
        compiler_params=_cparams(("arbitrary",), 48),
        name="combine",
    )(i32(win), i32(fast), i32(bounds), growt, ye, x1, mod3, g_post2)


def _rope_tables(n):
    rows = n // GRID_W
    row = jnp.repeat(jnp.arange(rows), GRID_W).astype(F32)
    col = jnp.tile(jnp.arange(GRID_W), rows).astype(F32)
    inv = ROPE_BASE ** (-jnp.arange(0, HD // 2, 2, dtype=F32) / (HD // 2))
    ang = jnp.concatenate([row[:, None] * inv, col[:, None] * inv], axis=-1)
    cos, sin = jnp.cos(ang), jnp.sin(ang)
    return jnp.tile(cos, (1, 4)), jnp.tile(jnp.concatenate([-sin, sin], axis=-1), (1, 2))


def kernel(x_prompt, x_sample, cache_k, cache_v, state_rnn, c, c_ctx, w_mod, b_mod, g_pre1, g_post1, g_pre2, g_post2, w_in, conv_w, conv_b, lru_wr, lru_br, lru_wi, lru_bi, lru_lam, lam_q1, lam_k1, lam_q2, lam_k2, g_sub, w_a_out, w_b_out, w_o, w_router, w_e_gate, w_e_up, w_e_down):
    nbc, nc, _ = x_prompt.shape
    nbl, nl, _ = x_sample.shape
    assert nbc * nc == GROUP_TOKENS and nl == GROUP_TOKENS and w_mod.shape[0] == 1
    tc = nbc * nc
    x_c, x_l = x_prompt.reshape(tc, D_MODEL), x_sample.reshape(nbl * nl, D_MODEL)
    sq0 = lambda a: a.reshape(a.shape[1:])
    sq1 = lambda a: a.reshape(a.shape[:1] + a.shape[2:])
    (w_mod, w_in, conv_w, lru_wr, lru_wi, lru_br, lru_bi, lru_lam, w_a_out, w_b_out, w_o, w_router, w_e_gate, w_e_up,
     w_e_down) = map(sq0, (w_mod, w_in, conv_w, lru_wr, lru_wi, lru_br, lru_bi, lru_lam, w_a_out, w_b_out, w_o,
                           w_router, w_e_gate, w_e_up, w_e_down))
    state_rnn, cache_k, cache_v = map(sq1, (state_rnn, cache_k, cache_v))

    cvec = jnp.concatenate([c_ctx[None, :], c, jnp.zeros((SUBLANES - 1 - nbl, D_MODEL), F32)], axis=0)
    mod = _modulation(cvec, w_mod, b_mod)
    mod3 = mod.reshape(SUBLANES, 1, 6 * D_MODEL)

    proj = _in_proj(x_c, x_l, mod3, g_pre1, w_in.astype(BF16))

    rnn_w = (conv_w, conv_b, lru_wr, lru_wi, lru_br, lru_bi, lru_lam)
    ya_c, h_fin = _rnn_branch(proj, *rnn_w, jnp.zeros((nbc, 2, D_RNN), F32), seq=nc, nseq=SUBLANES,
                              nb=nbc // SUBLANES, row0=0)
    ya_l, _ = _rnn_branch(proj, *rnn_w, state_rnn, seq=nl, nseq=1, nb=nbl, row0=tc // nl)

    lam_vecs = (lam_q1, lam_k1, lam_q2, lam_k2)
    on_c = _attention(proj, lam_vecs, g_sub, n=nc, nb=nbc, row0=0, tq=nc)
    past = cache_k.shape[1]
    on_l = _attention(proj, lam_vecs, g_sub, n=nl, nb=nbl, row0=tc // nl, tq=512,
                      cache=(cache_k.reshape(nbl, past, N_HEADS * VD), cache_v.reshape(nbl, past, N_HEADS * VD)),
                      tables=_rope_tables(nl))

    x1, h2, aff_t = _post_mix(ya_c, ya_l, on_c, on_l, proj, x_c, x_l, mod3, g_post1, g_pre2, w_a_out.astype(BF16),
                              w_b_out.astype(BF16), w_o.astype(BF16), w_router.T)

    capc = EC_FACTOR * nc // N_EXPERTS
    idx_c, val_c, growt_c = _route(aff_t, n=nc, nsets=nbc, col0=0, row_stride=capc)
    idx_l, val_l, growt_l = _route(aff_t, n=nl, nsets=nbl, col0=tc // nl, row_stride=0)
    gid_c = idx_c[..., 0] + (jnp.arange(nbc, dtype=jnp.int32) * nc)[:, None, None]
    gid_l = idx_l[..., 0] + (tc + jnp.arange(nbl, dtype=jnp.int32) * nl)[:, None, None]
    gidx = jnp.concatenate([gid_c.transpose(1, 0, 2).reshape(N_EXPERTS, -1),
                            gid_l.transpose(1, 0, 2).reshape(N_EXPERTS, -1)], axis=1)
    vals = jnp.concatenate([val_c.transpose(1, 0, 2, 3).reshape(N_EXPERTS, -1, 1),
                            val_l.transpose(1, 0, 2, 3).reshape(N_EXPERTS, -1, 1)], axis=1)
    growt = jnp.concatenate([growt_c, growt_l], axis=0)

    ye = _experts(gidx, h2, vals, w_e_gate, w_e_up, w_e_down)
    y_c, y_l = _combine(growt, ye, x1, mod3, g_post2, tc)

    y_prompt = y_c.reshape(nbc, nc, D_MODEL)
    y_sample = y_l.reshape(nbl, nl, D_MODEL)
    kblk = (2 * D_RNN + N_HEADS * VD) // (N_HEADS * VD)
    new_k = _take_cols(proj, tc, kblk, N_HEADS * VD).reshape(nbc, 1, nc, N_HEADS, 2 * HD)
    new_v = _take_cols(proj, tc, kblk + 1, N_HEADS * VD).reshape(nbc, 1, nc, N_HEADS, VD)
    return (y_prompt, y_sample, new_k, new_v, h_fin[:, None])
```

```python
import functools
import math

import jax
import jax.numpy as jnp
from jax import lax
from jax.experimental import pallas as pl
from jax.experimental.pallas import tpu as pltpu

F32 = jnp.float32
BF16 = jnp.bfloat16
HIGHEST = lax.Precision.HIGHEST

D_MODEL = 2048
D_RNN = 1024
RNN_BLOCKS = 8
RNN_BS = 128
CONV_W = 4
LRU_C = 8.0
N_HEADS = 8
HD = 64
VD = 128
N_EXPERTS = 16
D_EXPERT = 1024
EC_FACTOR = 2
EPS = 1e-6
IN_COLS = 9216
GRID_W = 64
ROPE_BASE = 10000.0
LAM_INIT = 0.8 - 0.6 * math.exp(-0.3 * 0)

GROUP_TOKENS = 4096
GROUP_ROWS = 512
SUBLANES = 8
LANES = 128
V7X_VMEM_BYTES = 64 * 1024 * 1024


def _cparams(semantics, vmem_mb):
    return pltpu.CompilerParams(dimension_semantics=semantics, vmem_limit_bytes=vmem_mb * 1024 * 1024)


def _sigmoid(x):
    return 0.5 * jnp.tanh(0.5 * x) + 0.5


def _rms(x, g):
    return x * lax.rsqrt(jnp.mean(x * x, axis=-1, keepdims=True) + EPS) * g


def _mod_kernel(c_ref, w_ref, b_ref, o_ref):
    c = c_ref[...]
    s = c * _sigmoid(c)
    o_ref[...] = jnp.dot(s, w_ref[...], precision=HIGHEST, preferred_element_type=F32) + b_ref[...]


def _modulation(cvec, w_mod, b_mod):
    tn = 1024
    return pl.pallas_call(
        _mod_kernel,
        grid=(6 * D_MODEL // tn,),
        in_specs=[
            pl.BlockSpec((SUBLANES, D_MODEL), lambda j: (0, 0)),
            pl.BlockSpec((D_MODEL, tn), lambda j: (0, j)),
            pl.BlockSpec((1, tn), lambda j: (0, j)),
        ],
        out_specs=pl.BlockSpec((SUBLANES, tn), lambda j: (0, j)),
        out_shape=jax.ShapeDtypeStruct((SUBLANES, 6 * D_MODEL), F32),
        compiler_params=_cparams(("arbitrary",), 40),
        name="modulation",
    )(cvec, w_mod, b_mod)


def _in_kernel(xc_ref, xl_ref, sh_ref, sc_ref, g_ref, w_ref, o_ref, h_ref, *, ctx_tiles):
    i, j = pl.program_id(0), pl.program_id(1)
    rows = 128

    def prologue(x_ref):
        def body(c, carry):
            r0 = pl.multiple_of(c * rows, rows)
            h = _rms(x_ref[pl.ds(r0, rows), :], g_ref[...]) * (1.0 + sc_ref[...]) + sh_ref[...]
            h_ref[pl.ds(r0, rows), :] = h.astype(BF16)
            return carry

        lax.fori_loop(0, h_ref.shape[0] // rows, body, 0)

    @pl.when((j == 0) & (i < ctx_tiles))
    def _():
        prologue(xc_ref)

    @pl.when((j == 0) & (i >= ctx_tiles))
    def _():
        prologue(xl_ref)

    o_ref[...] = jnp.dot(h_ref[...], w_ref[...], preferred_element_type=F32)


def _in_proj(x_c, x_l, mod3, g_pre1, w_in_bf):
    T = x_c.shape[0] + x_l.shape[0]
    tm, tn = 1024, 1024
    tpg = GROUP_TOKENS // tm
    nct = x_c.shape[0] // tm
    return pl.pallas_call(
        functools.partial(_in_kernel, ctx_tiles=nct),
        grid=(T // tm, IN_COLS // tn),
        in_specs=[
            pl.BlockSpec((tm, D_MODEL), lambda i, j: (jnp.minimum(i, nct - 1), 0), pipeline_mode=pl.Buffered(1)),
            pl.BlockSpec((tm, D_MODEL), lambda i, j: (jnp.maximum(i - nct, 0), 0)),
            pl.BlockSpec((None, 1, D_MODEL), lambda i, j: (i // tpg, 0, 0)),
            pl.BlockSpec((None, 1, D_MODEL), lambda i, j: (i // tpg, 0, 1)),
            pl.BlockSpec((1, D_MODEL), lambda i, j: (0, 0)),
            pl.BlockSpec((D_MODEL, tn), lambda i, j: (0, j)),
        ],
        out_specs=pl.BlockSpec((tm, tn), lambda i, j: (i, j)),
        out_shape=jax.ShapeDtypeStruct((T, IN_COLS), F32),
        scratch_shapes=[pltpu.VMEM((tm, D_MODEL), BF16)],
        compiler_params=_cparams(("arbitrary", "arbitrary"), 56),
        name="in_proj",
    )(x_c, x_l, mod3, mod3, g_pre1, w_in_bf)


def _copy_kernel(x_ref, o_ref):
    o_ref[...] = x_ref[...]


def _take_cols(proj, rows, col_block, width):
    tm = 1024
    return pl.pallas_call(
        _copy_kernel,
        grid=(rows // tm,),
        in_specs=[pl.BlockSpec((tm, width), lambda i: (i, col_block))],
        out_specs=pl.BlockSpec((tm, width), lambda i: (i, 0)),
        out_shape=jax.ShapeDtypeStruct((rows, width), proj.dtype),
        compiler_params=_cparams(("arbitrary",), 40),
        name="take_cols",
    )(proj)


def _gelu_tanh(x):
    return 0.5 * x * (1.0 + jnp.tanh(math.sqrt(2.0 / math.pi) * (x + 0.044715 * (x * x * x))))


def _softplus(x):
    return jnp.maximum(x, 0.0) + jnp.log(1.0 + jnp.exp(-jnp.abs(x)))


def _split_bf16(x):
    hi = x.astype(BF16)
    return hi, (x - hi.astype(F32)).astype(BF16)


def _dot_3pass(x, w_hi, w_lo):
    x_hi, x_lo = _split_bf16(x)
    d = functools.partial(jnp.dot, preferred_element_type=F32)
    return d(x_hi, w_hi) + (d(x_lo, w_hi) + d(x_hi, w_lo))


def _rnn_kernel(xr_ref, gr_ref, cw_ref, cb_ref, wr_ref, wi_ref, br_ref, bi_ref, lam_ref, h0_ref,
                ya_ref, hf_ref, xp_ref, a_ref, b_ref, h_ref, p_ref, *, seq, nseq):
    n = seq * nseq
    seg = n // SUBLANES
    pitch = seg + SUBLANES
    zeros8 = jnp.zeros((SUBLANES, LANES), F32)
    xp_ref[0:SUBLANES, :] = zeros8
    xp_ref[SUBLANES + n:2 * SUBLANES + n, :] = zeros8
    xp_ref[SUBLANES:SUBLANES + n, :] = xr_ref[...]
    cw = cw_ref[...]
    cb = cb_ref[...]
    sp = _softplus(-lam_ref[...])
    w_r = [_split_bf16(wr_ref[d]) for d in range(2)]
    w_i = [_split_bf16(wi_ref[d]) for d in range(2)]

    def gates(s, carry):
        r0 = pl.multiple_of(s * seg, SUBLANES)
        taps = [xp_ref[pl.ds(r0 + 6 + k, seg), :] for k in range(CONV_W)]
        if nseq > 1:
            t = (r0 + lax.broadcasted_iota(jnp.int32, (seg, LANES), 0)) % seq
            taps[0] = jnp.where(t >= 2, taps[0], 0.0)
            taps[1] = jnp.where(t >= 1, taps[1], 0.0)
            taps[3] = jnp.where(t <= seq - 2, taps[3], 0.0)
        xc = cb + taps[0] * cw[0:1]
        xc = xc + taps[1] * cw[1:2]
        xc = xc + taps[2] * cw[2:3]
        xc = xc + taps[3] * cw[3:4]
        p0 = pl.multiple_of(s * pitch, SUBLANES)
        for d in range(2):
            r = _sigmoid(_dot_3pass(xc, *w_r[d]) + br_ref[d:d + 1, :])
            i = _sigmoid(_dot_3pass(xc, *w_i[d]) + bi_ref[d:d + 1, :])
            a = jnp.exp((-LRU_C) * r * sp[d:d + 1, :])
            y = 1.0 - a * a
            a_ref[d, pl.ds(p0, seg), :] = a
            b_ref[d, pl.ds(p0, seg), :] = jnp.where(y > 0.0, y * lax.rsqrt(y), 0.0) * (i * xc)
        return carry

    lax.fori_loop(0, SUBLANES, gates, 0)

    def scan(j, carry):
        hf, af, hb, ab = carry
        a0 = a_ref[0, pl.ds(j, SUBLANES, stride=pitch), :]
        b0 = b_ref[0, pl.ds(j, SUBLANES, stride=pitch), :]
        hf = a0 * hf + b0
        af = a0 * af
        h_ref[0, pl.ds(j, SUBLANES, stride=pitch), :] = hf
        p_ref[0, pl.ds(j, SUBLANES, stride=pitch), :] = af
        jb = seg - 1 - j
        a1 = a_ref[1, pl.ds(jb, SUBLANES, stride=pitch), :]
        b1 = b_ref[1, pl.ds(jb, SUBLANES, stride=pitch), :]
        hb = a1 * hb + b1
        ab = a1 * ab
        h_ref[1, pl.ds(jb, SUBLANES, stride=pitch), :] = hb
        p_ref[1, pl.ds(jb, SUBLANES, stride=pitch), :] = ab
        return hf, af, hb, ab

    ones8 = jnp.ones((SUBLANES, LANES), F32)
    ef, pf, eb, pb = lax.fori_loop(0, seg, scan, (zeros8, ones8, zeros8, ones8), unroll=8)

    spq = SUBLANES // nseq
    cf = [None] * SUBLANES
    cbk = [None] * SUBLANES
    for q in range(nseq):
        h0 = h0_ref[q]
        first, last = q * spq, q * spq + spq - 1
        cf[first] = h0[0:1, :]
        for s in range(first + 1, last + 1):
            cf[s] = ef[s - 1:s, :] + pf[s - 1:s, :] * cf[s - 1]
        cbk[last] = h0[1:2, :]
        for s in range(last - 1, first - 1, -1):
            cbk[s] = eb[s + 1:s + 2, :] + pb[s + 1:s + 2, :] * cbk[s + 1]
        hf_ref[q, 0:1, :] = ef[last:last + 1, :] + pf[last:last + 1, :] * cf[last]
        hf_ref[q, 1:2, :] = eb[first:first + 1, :] + pb[first:first + 1, :] * cbk[first]

    for s in range(SUBLANES):
        hfw = h_ref[0, s * pitch:s * pitch + seg, :] + p_ref[0, s * pitch:s * pitch + seg, :] * cf[s]
        hbw = h_ref[1, s * pitch:s * pitch + seg, :] + p_ref[1, s * pitch:s * pitch + seg, :] * cbk[s]
        y = (hfw + hbw) * _gelu_tanh(gr_ref[s * seg:(s + 1) * seg, :])
        ya_ref[s * seg:(s + 1) * seg, :] = y.astype(ya_ref.dtype)


def _rnn_branch(proj, conv_w, conv_b, wr, wi, br, bi, lam, h0, *, seq, nseq, nb, row0):
    n = seq * nseq
    seg = n // SUBLANES
    pitch = seg + SUBLANES
    return pl.pallas_call(
        functools.partial(_rnn_kernel, seq=seq, nseq=nseq),
        grid=(nb, RNN_BLOCKS),
        in_specs=[
            pl.BlockSpec((n, RNN_BS), lambda b, k: (row0 + b, k)),
            pl.BlockSpec((n, RNN_BS), lambda b, k: (row0 + b, RNN_BLOCKS + k)),
            pl.BlockSpec((CONV_W, RNN_BS), lambda b, k: (0, k)),
            pl.BlockSpec((1, RNN_BS), lambda b, k: (0, k)),
            pl.BlockSpec((2, None, RNN_BS, RNN_BS), lambda b, k: (0, k, 0, 0)),
            pl.BlockSpec((2, None, RNN_BS, RNN_BS), lambda b, k: (0, k, 0, 0)),
            pl.BlockSpec((2, RNN_BS), lambda b, k: (0, k)),
            pl.BlockSpec((2, RNN_BS), lambda b, k: (0, k)),
            pl.BlockSpec((2, RNN_BS), lambda b, k: (0, k)),
            pl.BlockSpec((nseq, 2, RNN_BS), lambda b, k: (b, 0, k)),
        ],
        out_specs=[
            pl.BlockSpec((n, RNN_BS), lambda b, k: (b, k)),
            pl.BlockSpec((nseq, 2, RNN_BS), lambda b, k: (b, 0, k)),
        ],
        out_shape=[jax.ShapeDtypeStruct((nb * n, D_RNN), BF16), jax.ShapeDtypeStruct((nb * nseq, 2, D_RNN), F32)],
        scratch_shapes=[
            pltpu.VMEM((n + 2 * SUBLANES, RNN_BS), F32),
        ] + [pltpu.VMEM((2, SUBLANES * pitch, RNN_BS), F32)] * 4,
        compiler_params=_cparams(("arbitrary", "arbitrary"), 40),
        name=f"rnn_n{n}",
    )(proj, proj, conv_w, conv_b, wr, wi, br, bi, lam, h0)


def _rope(x, cos, sin_signed):
    lane = lax.broadcasted_iota(jnp.int32, x.shape, 1)
    partner = jnp.where((lane % HD) < HD // 2, pltpu.roll(x, LANES - HD // 2, 1), pltpu.roll(x, HD // 2, 1))
    return x * cos + partner * sin_signed


ATTN_SUB = 128
ATTN_LOOKAHEAD = 4


def _attn_kernel(*refs, n, npast, tq, rope):
    if rope:
        (lq1, lk1, lq2, lk2, q_ref, k_ref, v_ref, gsub_ref, ck_ref, cv_ref, cosk_ref, sink_ref, cosq_ref, sinq_ref,
         o_ref, kb_ref, vb_ref, s_ref, p_ref) = refs
    else:
        lq1, lk1, lq2, lk2, q_ref, k_ref, v_ref, gsub_ref, o_ref, kb_ref, vb_ref, s_ref, p_ref = refs
    chunk = 256
    nk = n + npast

    @pl.when(pl.program_id(2) == 0)
    def _():
        ones_col = jnp.where(lax.broadcasted_iota(jnp.int32, (chunk, VD), 1) == 0, 1.0, 0.0).astype(BF16)
        if npast:
            for r0 in range(0, npast, chunk):
                kb_ref[:, r0:r0 + chunk] = ck_ref[r0:r0 + chunk, :].T.astype(BF16)
                vb_ref[r0:r0 + chunk, VD:2 * VD] = ones_col
            vb_ref[0:npast, 0:VD] = cv_ref[...].astype(BF16)

        for r0 in range(0, n, chunk):
            k = k_ref[r0:r0 + chunk, :]
            if rope:
                k = _rope(k, cosk_ref[r0:r0 + chunk, :], sink_ref[r0:r0 + chunk, :])
            kb_ref[:, npast + r0:npast + r0 + chunk] = k.T.astype(BF16)
            vb_ref[npast + r0:npast + r0 + chunk, 0:VD] = v_ref[r0:r0 + chunk, :].astype(BF16)
            vb_ref[npast + r0:npast + r0 + chunk, VD:2 * VD] = ones_col

    lam = (jnp.exp(jnp.sum(lq1[...] * lk1[...], axis=-1, keepdims=True))
           - jnp.exp(jnp.sum(lq2[...] * lk2[...], axis=-1, keepdims=True)) + LAM_INIT)
    q = q_ref[...]
    if rope:
        q = _rope(q, cosq_ref[...], sinq_ref[...])
    q = q * (HD ** -0.5 * math.log2(math.e))
    lane = lax.broadcasted_iota(jnp.int32, (ATTN_SUB, VD), 1)
    nt = (((1,), (1,)), ((), ()))
    kc = 2 * LANES
    key_chunks = [(c0, min(kc, nk - c0)) for c0 in range(0, nk, kc)]
    nunits = 2 * (tq // ATTN_SUB)

    def scores(unit):
        sub, comp = divmod(unit, 2)
        qs = q[sub * ATTN_SUB:(sub + 1) * ATTN_SUB, :]
        qc = jnp.where((lane < HD) == (comp == 0), qs, 0.0).astype(BF16)
        m_run = None
        for c0, w in key_chunks:
            s = jnp.dot(qc, kb_ref[:, c0:c0 + w], preferred_element_type=F32)
            s_ref[unit, :, c0:c0 + w] = s
            for l0 in range(0, w, LANES):
                piece = s[:, l0:l0 + LANES]
                m_run = piece if m_run is None else jnp.maximum(m_run, piece)
        return jnp.max(m_run, axis=-1, keepdims=True)

    def values(unit, m):
        for c0, w in key_chunks:
            p_ref[unit, :, c0:c0 + w] = jnp.exp2(s_ref[unit, :, c0:c0 + w] - m).astype(BF16)
        return jnp.dot(p_ref[unit], vb_ref[...], preferred_element_type=F32)

    outs = [None] * nunits
    ahead = ATTN_LOOKAHEAD
    ms = [scores(u) for u in range(min(ahead, nunits))]
    for unit in range(nunits):
        if unit + ahead < nunits:
            ms.append(scores(unit + ahead))
        outs[unit] = values(unit, ms[unit])
        if unit % 2 == 1:
            sub = unit // 2
            o1, o2 = outs[unit - 1], outs[unit]
            o = o1[:, 0:VD] * (1.0 / o1[:, VD:VD + 1]) - o2[:, 0:VD] * (lam / o2[:, VD:VD + 1])
            o = _rms(o, gsub_ref[...]) * (1.0 - LAM_INIT)
            o_ref[sub * ATTN_SUB:(sub + 1) * ATTN_SUB, :] = o.astype(o_ref.dtype)


def _attention(proj, lam_vecs, g_sub, *, n, nb, row0, tq, cache=None, tables=None):
    rope = tables is not None
    npast = 0 if cache is None else cache[0].shape[1]
    nk = n + npast
    qpr = n // tq
    qcol, kcol, vcol = 2 * RNN_BLOCKS, 2 * RNN_BLOCKS + N_HEADS, 2 * RNN_BLOCKS + 2 * N_HEADS
    vec = pl.BlockSpec((1, HD), lambda b, h, t: (0, 0))
    in_specs = [vec, vec, vec, vec,
                pl.BlockSpec((tq, VD), lambda b, h, t: ((row0 + b) * qpr + t, qcol + h)),
                pl.BlockSpec((n, VD), lambda b, h, t: (row0 + b, kcol + h)),
                pl.BlockSpec((n, VD), lambda b, h, t: (row0 + b, vcol + h)),
                pl.BlockSpec((1, VD), lambda b, h, t: (0, 0))]
    args = [*lam_vecs, proj, proj, proj, g_sub]
    if rope:
        ck, cv = cache
        cos_t, sin_t = tables
        in_specs += [pl.BlockSpec((None, npast, VD), lambda b, h, t: (b, 0, h)),
                     pl.BlockSpec((None, npast, VD), lambda b, h, t: (b, 0, h)),
                     pl.BlockSpec((n, VD), lambda b, h, t: (0, 0), pipeline_mode=pl.Buffered(1)),
                     pl.BlockSpec((n, VD), lambda b, h, t: (0, 0), pipeline_mode=pl.Buffered(1)),
                     pl.BlockSpec((tq, VD), lambda b, h, t: (t, 0)),
                     pl.BlockSpec((tq, VD), lambda b, h, t: (t, 0))]
        args += [ck, cv, cos_t, sin_t, cos_t, sin_t]
    return pl.pallas_call(
        functools.partial(_attn_kernel, n=n, npast=npast, tq=tq, rope=rope),
        grid=(nb, N_HEADS, qpr),
        in_specs=in_specs,
        out_specs=pl.BlockSpec((tq, VD), lambda b, h, t: (b * qpr + t, h)),
        out_shape=jax.ShapeDtypeStruct((nb * n, N_HEADS * VD), BF16),
        scratch_shapes=[pltpu.VMEM((VD, nk), BF16), pltpu.VMEM((nk, 2 * VD), BF16),
                        pltpu.VMEM((2 * tq // ATTN_SUB, ATTN_SUB, nk), F32),
                        pltpu.VMEM((2 * tq // ATTN_SUB, ATTN_SUB, nk), BF16)],
        compiler_params=_cparams(("arbitrary", "arbitrary", "arbitrary"), 56),
        name=f"attn_n{n}",
    )(*args)


POST_TM = 512
POST_UNIT = 256


def _merge_kernel(yac_ref, yal_ref, onc_ref, onl_ref, ga0_ref, ga1_ref, gb0_ref, gb1_ref, wa_ref, wb_ref, m_ref,
                  *, ctx_tiles):
    is_ctx = pl.program_id(0) < ctx_tiles
    half = D_MODEL // 2
    for r0 in range(0, POST_TM, POST_UNIT):
        rs = slice(r0, r0 + POST_UNIT)
        ya = jnp.where(is_ctx, yac_ref[rs, :], yal_ref[rs, :])
        on = jnp.where(is_ctx, onc_ref[rs, :], onl_ref[rs, :])
        oa = jnp.dot(ya, wa_ref[...], preferred_element_type=F32)
        ob = jnp.dot(on, wb_ref[...], preferred_element_type=F32)
        m_ref[rs, 0:half] = (_sigmoid(ga0_ref[rs, :]) * oa[:, 0:half]
                             + _sigmoid(gb0_ref[rs, :]) * ob[:, 0:half]).astype(BF16)
        m_ref[rs, half:] = (_sigmoid(ga1_ref[rs, :]) * oa[:, half:]
                            + _sigmoid(gb1_ref[rs, :]) * ob[:, half:]).astype(BF16)


def _outproj_kernel(m_ref, xc_ref, xl_ref, g1_ref, sh2_ref, sc2_ref, gpost1_ref, gpre2_ref, wo_ref, wrt_ref,
                    x1_ref, h2_ref, aff_ref, *, ctx_tiles):
    is_ctx = pl.program_id(0) < ctx_tiles
    nt = functools.partial(lax.dot_general, dimension_numbers=(((1,), (1,)), ((), ())), preferred_element_type=F32)
    w_hi, w_lo = _split_bf16(wrt_ref[...])
    units = [slice(r0, r0 + POST_UNIT) for r0 in range(0, POST_TM, POST_UNIT)]
    ts = [jnp.dot(m_ref[rs, :], wo_ref[...], preferred_element_type=F32) for rs in units]
    for rs, t in zip(units, ts):
        x1 = jnp.where(is_ctx, xc_ref[rs, :], xl_ref[rs, :]) + g1_ref[...] * _rms(t, gpost1_ref[...])
        x1_ref[rs, :] = x1
        h2 = _rms(x1, gpre2_ref[...]) * (1.0 + sc2_ref[...]) + sh2_ref[...]
        h2_ref[rs, :] = h2
        h_hi, h_lo = _split_bf16(h2)
        logits = nt(w_hi, h_hi) + (nt(w_lo, h_hi) + nt(w_hi, h_lo))
        e = jnp.exp(logits - jnp.max(logits, axis=0, keepdims=True))
        aff_ref[:, rs] = e / jnp.sum(e, axis=0, keepdims=True)


def _post_mix(ya_c, ya_l, on_c, on_l, proj, x_c, x_l, mod3, g_post1, g_pre2, wa, wb, wo, wrt):
    T = x_c.shape[0] + x_l.shape[0]
    tm = POST_TM
    tpg = GROUP_TOKENS // tm
    nct = ya_c.shape[0] // tm
    ctx_blk = lambda w: pl.BlockSpec((tm, w), lambda i: (jnp.minimum(i, nct - 1), 0), pipeline_mode=pl.Buffered(1))
    lat_blk = lambda w: pl.BlockSpec((tm, w), lambda i: (jnp.maximum(i - nct, 0), 0))
    half = D_MODEL // 2
    gate0 = (2 * D_RNN + 2 * N_HEADS * VD + N_HEADS * VD) // half
    row = lambda c: pl.BlockSpec((None, 1, D_MODEL), lambda i: (i // tpg, 0, c))
    const = lambda shape: pl.BlockSpec(shape, lambda i: (0,) * len(shape), pipeline_mode=pl.Buffered(1))
    merged = pl.pallas_call(
        functools.partial(_merge_kernel, ctx_tiles=nct),
        grid=(T // tm,),
        in_specs=[
            ctx_blk(D_RNN), lat_blk(D_RNN), ctx_blk(N_HEADS * VD), lat_blk(N_HEADS * VD),
            pl.BlockSpec((tm, half), lambda i: (i, gate0)),
            pl.BlockSpec((tm, half), lambda i: (i, gate0 + 1)),
            pl.BlockSpec((tm, half), lambda i: (i, gate0 + 2)),
            pl.BlockSpec((tm, half), lambda i: (i, gate0 + 3)),
            const((D_RNN, D_MODEL)), const((N_HEADS * VD, D_MODEL)),
        ],
        out_specs=pl.BlockSpec((tm, D_MODEL), lambda i: (i, 0)),
        out_shape=jax.ShapeDtypeStruct((T, D_MODEL), BF16),
        compiler_params=_cparams(("arbitrary",), 56),
        name="merge",
    )(ya_c, ya_l, on_c, on_l, proj, proj, proj, proj, wa, wb)
    return pl.pallas_call(
        functools.partial(_outproj_kernel, ctx_tiles=nct),
        grid=(T // tm,),
        in_specs=[
            pl.BlockSpec((tm, D_MODEL), lambda i: (i, 0)),
            ctx_blk(D_MODEL), lat_blk(D_MODEL),
            row(2), row(3), row(4),
            const((1, D_MODEL)), const((1, D_MODEL)),
            const((D_MODEL, D_MODEL)),
            const((N_EXPERTS, D_MODEL)),
        ],
        out_specs=[
            pl.BlockSpec((tm, D_MODEL), lambda i: (i, 0)),
            pl.BlockSpec((tm, D_MODEL), lambda i: (i, 0)),
            pl.BlockSpec((N_EXPERTS, tm), lambda i: (0, i)),
        ],
        out_shape=[jax.ShapeDtypeStruct((T, D_MODEL), F32), jax.ShapeDtypeStruct((T, D_MODEL), F32),
                   jax.ShapeDtypeStruct((N_EXPERTS, T), F32)],
        compiler_params=_cparams(("arbitrary",), 56),
        name="out_proj",
    )(merged, x_c, x_l, mod3, mod3, mod3, g_post1, g_pre2, wo, wrt)


def _cumsum_excl(x, tri):
    n = x.shape[1]
    blk = tri.shape[0]
    outs = []
    carry = jnp.zeros((x.shape[0], 1), F32)
    for c in range(n // blk):
        xc = x[:, c * blk:(c + 1) * blk]
        outs.append(jnp.dot(xc.astype(BF16), tri, preferred_element_type=F32) + carry)
        carry = carry + jnp.sum(xc, axis=1, keepdims=True)
    return outs[0] if len(outs) == 1 else jnp.concatenate(outs, axis=1)


def _route_kernel(aff_ref, idx_ref, val_ref, growt_ref, *, n, cap, row_stride):
    aff = aff_ref[...]

    def bisect(_, lohi):
        lo, hi = lohi
        mid = lo + ((hi - lo) >> 1)
        cnt = jnp.sum((aff >= pltpu.bitcast(mid, F32)).astype(jnp.int32), axis=1, keepdims=True)
        ok = cnt >= cap
        return jnp.where(ok, mid, lo), jnp.where(ok, hi, mid)

    lo0 = jnp.zeros((N_EXPERTS, 1), jnp.int32)
    hi0 = jnp.full((N_EXPERTS, 1), 0x7F800000, jnp.int32)
    thr, _ = lax.fori_loop(0, 31, bisect, (lo0, hi0))
    gt = aff >= pltpu.bitcast(thr + 1, F32)
    eq = (aff >= pltpu.bitcast(thr, F32)) & jnp.logical_not(gt)
    need = cap - jnp.sum(gt.astype(jnp.int32), axis=1, keepdims=True)
    blk = min(n, 256)
    tri = (lax.broadcasted_iota(jnp.int32, (blk, blk), 0) < lax.broadcasted_iota(jnp.int32, (blk, blk), 1)).astype(BF16)
    eqpos = _cumsum_excl(eq.astype(F32), tri)
    sel = gt | (eq & (eqpos < need.astype(F32)))
    pos = _cumsum_excl(sel.astype(F32), tri).astype(jnp.int32)
    slot = jnp.where(sel, pos, -1)
    grow = jnp.where(sel, pos + pl.program_id(0) * row_stride, -1)
    growt_ref[...] = grow.astype(F32).T.astype(jnp.int32)

    rc = min(cap, 32)
    tok = lax.broadcasted_iota(jnp.int32, (rc, n), 1)
    for e in range(N_EXPERTS):
        slot_e = slot[e:e + 1, :]
        aff_e = aff[e:e + 1, :]

        def extract(c, carry):
            r0 = pl.multiple_of(c * rc, rc)
            hit = slot_e == (lax.broadcasted_iota(jnp.int32, (rc, n), 0) + r0)
            idx_ref[e, pl.ds(r0, rc), :] = jnp.sum(jnp.where(hit, tok, 0), axis=1, keepdims=True)
            val_ref[e, pl.ds(r0, rc), :] = jnp.sum(jnp.where(hit, aff_e, 0.0), axis=1, keepdims=True)
            return carry

        lax.fori_loop(0, cap // rc, extract, 0)


def _route(aff_t, *, n, nsets, col0, row_stride):
    cap = EC_FACTOR * n // N_EXPERTS
    kern = functools.partial(_route_kernel, n=n, cap=cap, row_stride=row_stride)
    return pl.pallas_call(
        kern,
        grid=(nsets,),
        in_specs=[pl.BlockSpec((N_EXPERTS, n), lambda s: (0, col0 + s))],
        out_specs=[
            pl.BlockSpec((None, N_EXPERTS, cap, 1), lambda s: (s, 0, 0, 0)),
            pl.BlockSpec((None, N_EXPERTS, cap, 1), lambda s: (s, 0, 0, 0)),
            pl.BlockSpec((n, N_EXPERTS), lambda s: (s, 0)),
        ],
        out_shape=[jax.ShapeDtypeStruct((nsets, N_EXPERTS, cap, 1), jnp.int32),
                   jax.ShapeDtypeStruct((nsets, N_EXPERTS, cap, 1), F32),
                   jax.ShapeDtypeStruct((nsets * n, N_EXPERTS), jnp.int32)],
        compiler_params=_cparams(("arbitrary",), 48),
        name=f"route_n{n}",
    )(aff_t)


WEIGHT_PARTS = 4


def _expert_kernel(cur_ref, nxt_ref, h2_hbm, val_ref, wg_hbm, wu_hbm, wd_hbm, ye_ref, xs0_ref, xs1_ref,
                   wg_ref, wu_ref, wd_ref, sg_ref, su_ref, sd_ref, sem, wsem, *, rows):
    e, grp = pl.program_id(0), pl.program_id(1)
    step = e * pl.num_programs(1) + grp
    nsteps = pl.num_programs(0) * pl.num_programs(1)
    bufs = (xs0_ref, xs1_ref)
    rin, rdn = D_MODEL // WEIGHT_PARTS, D_EXPERT // WEIGHT_PARTS

    def slab_copies(ex, part):
        r_in = pl.multiple_of(part * rin, rin)
        r_dn = pl.multiple_of(part * rdn, rdn)
        return (pltpu.make_async_copy(wg_hbm.at[ex, pl.ds(r_in, rin), :], sg_ref, wsem.at[0]),
                pltpu.make_async_copy(wu_hbm.at[ex, pl.ds(r_in, rin), :], su_ref, wsem.at[1]),
                pltpu.make_async_copy(wd_hbm.at[ex, pl.ds(r_dn, rdn), :], sd_ref, wsem.at[2]))

    def fetch(ex, part):
        for cp in slab_copies(ex, part):
            cp.start()

    def land(ex, part):
        for cp in slab_copies(ex, part):
            cp.wait()
        st = ex % 2
        wg_ref[st, pl.ds(pl.multiple_of(part * rin, rin), rin), :] = sg_ref[...].astype(BF16)
        wu_ref[st, pl.ds(pl.multiple_of(part * rin, rin), rin), :] = su_ref[...].astype(BF16)
        wd_ref[st, pl.ds(pl.multiple_of(part * rdn, rdn), rdn), :] = sd_ref[...].astype(BF16)

    @pl.when(step == 0)
    def _():
        for part in range(WEIGHT_PARTS):
            fetch(0, part)
            land(0, part)

    has_next = e + 1 < pl.num_programs(0)

    @pl.when(has_next & (grp >= 1) & (grp <= WEIGHT_PARTS))
    def _():
        land(e + 1, grp - 1)

    @pl.when(has_next & (grp < WEIGHT_PARTS))
    def _():
        fetch(e + 1, grp)

    wset = e % 2

    def row_copy(idx_ref, r, s):
        return pltpu.make_async_copy(h2_hbm.at[pl.ds(idx_ref[0, r], 1), :], bufs[s].at[pl.ds(r, 1), :], sem.at[s])

    def slot_wait(s):
        pltpu.make_async_copy(h2_hbm.at[pl.ds(0, rows), :], bufs[s], sem.at[s]).wait()

    @pl.when(step == 0)
    def _():
        def issue(r, carry):
            row_copy(cur_ref, r, 0).start()
            return carry

        lax.fori_loop(0, rows, issue, 0, unroll=8)

    def body(s):
        slot_wait(s)
        for r in range(rows):
            row_copy(nxt_ref, r, 1 - s).start()
        half = rows // 2
        hids = []
        for c in range(2):
            xs = bufs[s][c * half:(c + 1) * half, :].astype(BF16)
            g = jnp.dot(xs, wg_ref[wset], preferred_element_type=F32)
            u = jnp.dot(xs, wu_ref[wset], preferred_element_type=F32)
            hids.append(((g * _sigmoid(g)) * u).astype(BF16))
        for c in range(2):
            y = jnp.dot(hids[c], wd_ref[wset], preferred_element_type=F32) * val_ref[c * half:(c + 1) * half, :]
            ye_ref[c * half:(c + 1) * half, :] = y.astype(ye_ref.dtype)

        @pl.when(step == nsteps - 1)
        def _():
            slot_wait(1 - s)

    for s in range(2):
        pl.when(step % 2 == s)(functools.partial(body, s))


def _experts(gidx, h2, vals, wg, wu, wd):
    rows_per_expert = vals.shape[1]
    rows = GROUP_ROWS
    groups = rows_per_expert // rows
    assert groups > WEIGHT_PARTS
    last = N_EXPERTS * groups - 1
    hbm = pl.BlockSpec(memory_space=pl.ANY)
    return pl.pallas_call(
        functools.partial(_expert_kernel, rows=rows),
        grid=(N_EXPERTS, groups),
        in_specs=[
            pl.BlockSpec((None, 1, rows), lambda e, g: (e * groups + g, 0, 0), memory_space=pltpu.SMEM),
            pl.BlockSpec((None, 1, rows), lambda e, g: (jnp.minimum(e * groups + g + 1, last), 0, 0),
                         memory_space=pltpu.SMEM),
            hbm,
            pl.BlockSpec((None, rows, 1), lambda e, g: (e, g, 0)),
            hbm, hbm, hbm,
        ],
        out_specs=pl.BlockSpec((None, rows, D_MODEL), lambda e, g: (e, g, 0)),
        out_shape=jax.ShapeDtypeStruct((N_EXPERTS, rows_per_expert, D_MODEL), BF16),
        scratch_shapes=[pltpu.VMEM((rows, D_MODEL), F32), pltpu.VMEM((rows, D_MODEL), F32),
                        pltpu.VMEM((2, D_MODEL, D_EXPERT), BF16), pltpu.VMEM((2, D_MODEL, D_EXPERT), BF16),
                        pltpu.VMEM((2, D_EXPERT, D_MODEL), BF16),
                        pltpu.VMEM((D_MODEL // WEIGHT_PARTS, D_EXPERT), F32),
                        pltpu.VMEM((D_MODEL // WEIGHT_PARTS, D_EXPERT), F32),
                        pltpu.VMEM((D_EXPERT // WEIGHT_PARTS, D_MODEL), F32),
                        pltpu.SemaphoreType.DMA((2,)), pltpu.SemaphoreType.DMA((3,))],
        compiler_params=_cparams(("arbitrary", "arbitrary"), 56),
        name="experts",
    )(gidx.reshape(N_EXPERTS * groups, 1, rows), gidx.reshape(N_EXPERTS * groups, 1, rows), h2, vals, wg, wu, wd)


COMBINE_TM = 256
COMBINE_WIN = 128
BF16_ROW_TILE = 16


def _combine_kernel(win_ref, fast_ref, bounds_ref, growt_ref, ye_hbm, x1_ref, g2_ref, gpost2_ref, oc_ref, ol_ref,
                    wbuf_ref, oh_ref, acc_ref, sbuf_ref, wsem, ssem, *, ctx_steps):
    i = pl.program_id(0)
    nsteps = pl.num_programs(0)
    tpg = GROUP_TOKENS // COMBINE_TM
    slot = i % 2
    half = GROUP_ROWS // 2
    growt = growt_ref[...]

    def window_copy(step, e, s):
        start = pl.multiple_of((step // tpg) * GROUP_ROWS + win_ref[step * N_EXPERTS + e], BF16_ROW_TILE)
        return pltpu.make_async_copy(ye_hbm.at[e, pl.ds(start, COMBINE_WIN), :],
                                     wbuf_ref.at[s, pl.ds(e * COMBINE_WIN, COMBINE_WIN), :], wsem.at[s])

    def fetch(step, s):
        for e in range(N_EXPERTS):
            window_copy(step, e, s).start()

    @pl.when((i == 0) & (fast_ref[0] == 1))
    def _():
        fetch(0, 0)

    nxt = jnp.minimum(i + 1, nsteps - 1)

    @pl.when((i + 1 < nsteps) & (fast_ref[nxt] == 1))
    def _():
        fetch(nxt, 1 - slot)

    @pl.when(fast_ref[i] == 1)
    def _():
        for e in range(N_EXPERTS):
            window_copy(i, e, slot).wait()
        for e in range(N_EXPERTS):
            rows = win_ref[i * N_EXPERTS + e] + lax.broadcasted_iota(jnp.int32, (COMBINE_TM, COMBINE_WIN), 1)
            oh_ref[:, e * COMBINE_WIN:(e + 1) * COMBINE_WIN] = jnp.where(growt[:, e:e + 1] == rows, 1.0, 0.0).astype(BF16)
        acc_ref[...] = jnp.dot(oh_ref[...], wbuf_ref[slot], preferred_element_type=F32)

    @pl.when(fast_ref[i] == 0)
    def _():
        acc_ref[...] = jnp.zeros_like(acc_ref)
        for e in range(N_EXPERTS):
            lo = bounds_ref[(i * N_EXPERTS + e) * 2]
            hi = bounds_ref[(i * N_EXPERTS + e) * 2 + 1]
            for c in range(2):
                @pl.when((lo < (c + 1) * half) & (hi > c * half))
                def _():
                    cp = pltpu.make_async_copy(ye_hbm.at[e, pl.ds((i // tpg) * GROUP_ROWS + c * half, half), :],
                                               sbuf_ref, ssem)
                    cp.start()
                    cp.wait()
                    rows = lax.broadcasted_iota(jnp.int32, (COMBINE_TM, half), 1) + c * half
                    onehot = jnp.where(growt[:, e:e + 1] == rows, 1.0, 0.0).astype(BF16)
                    acc_ref[...] += jnp.dot(onehot, sbuf_ref[...], preferred_element_type=F32)

    y = x1_ref[...] + g2_ref[...] * _rms(acc_ref[...], gpost2_ref[...])

    @pl.when(i < ctx_steps)
    def _():
        oc_ref[...] = y

    @pl.when(i >= ctx_steps)
    def _():
        ol_ref[...] = y


def _combine(growt, ye, x1, mod3, g_post2, ctx_tokens):
    T = x1.shape[0]
    tm = COMBINE_TM
    tpg = GROUP_TOKENS // tm
    g = growt.reshape(T // tm, tm, N_EXPERTS)
    lo = jnp.min(jnp.where(g >= 0, g, GROUP_ROWS), axis=1)
    hi = jnp.max(g, axis=1) + 1
    win = jnp.clip((lo // BF16_ROW_TILE) * BF16_ROW_TILE, 0, GROUP_ROWS - COMBINE_WIN)
    fast = jnp.all((hi <= lo) | (hi <= win + COMBINE_WIN), axis=1)
    bounds = jnp.stack([lo, hi], axis=-1)
    i32 = lambda a: a.reshape(-1).astype(jnp.int32)
    nct = ctx_tokens // tm
    return pl.pallas_call(
        functools.partial(_combine_kernel, ctx_steps=nct),
        grid_spec=pltpu.PrefetchScalarGridSpec(
            num_scalar_prefetch=3,
            grid=(T // tm,),
            in_specs=[
                pl.BlockSpec((tm, N_EXPERTS), lambda i, *_: (i, 0)),
                pl.BlockSpec(memory_space=pl.ANY),
                pl.BlockSpec((tm, D_MODEL), lambda i, *_: (i, 0)),
                pl.BlockSpec((None, 1, D_MODEL), lambda i, *_: (i // tpg, 0, 5)),
                pl.BlockSpec((1, D_MODEL), lambda i, *_: (0, 0)),
            ],
            out_specs=[pl.BlockSpec((tm, D_MODEL), lambda i, *_: (jnp.minimum(i, nct - 1), 0)),
                       pl.BlockSpec((tm, D_MODEL), lambda i, *_: (jnp.maximum(i - nct, 0), 0))],
            scratch_shapes=[
                pltpu.VMEM((2, N_EXPERTS * COMBINE_WIN, D_MODEL), BF16),
                pltpu.VMEM((tm, N_EXPERTS * COMBINE_WIN), BF16),
                pltpu.VMEM((tm, D_MODEL), F32),
                pltpu.VMEM((GROUP_ROWS // 2, D_MODEL), BF16),
                pltpu.SemaphoreType.DMA((2,)),
                pltpu.SemaphoreType.DMA(()),
            ],
        ),
        out_shape=[jax.ShapeDtypeStruct((ctx_tokens, D_MODEL), F32),
                   jax.ShapeDtypeStruct((T - ctx_tokens, D_MODEL), F32)],
        compiler_params=_cparams(("arbitrary",), 48),
        name="combine",
    )(i32(win), i32(fast), i32(bounds), growt, ye, x1, mod3, g_post2)


def _rope_tables(n):
    rows = n // GRID_W
    row = jnp.repeat(jnp.arange(rows), GRID_W).astype(F32)
    col = jnp.tile(jnp.arange(GRID_W), rows).astype(F32)
    inv = ROPE_BASE ** (-jnp.arange(0, HD // 2, 2, dtype=F32) / (HD // 2))
    ang = jnp.concatenate([row[:, None] * inv, col[:, None] * inv], axis=-1)
    cos, sin = jnp.cos(ang), jnp.sin(ang)
    return jnp.tile(cos, (1, 4)), jnp.tile(jnp.concatenate([-sin, sin], axis=-1), (1, 2))


def kernel(x_prompt, x_sample, cache_k, cache_v, state_rnn, c, c_ctx, w_mod, b_mod, g_pre1, g_post1, g_pre2, g_post2, w_in, conv_w, conv_b, lru_wr, lru_br, lru_wi, lru_bi, lru_lam, lam_q1, lam_k1, lam_q2, lam_k2, g_sub, w_a_out, w_b_out, w_o, w_router, w_e_gate, w_e_up, w_e_down):
    nbc, nc, _ = x_prompt.shape
    nbl, nl, _ = x_sample.shape
    assert nbc * nc == GROUP_TOKENS and nl == GROUP_TOKENS and w_mod.shape[0] == 1
    tc = nbc * nc
    x_c, x_l = x_prompt.reshape(tc, D_MODEL), x_sample.reshape(nbl * nl, D_MODEL)
    sq0 = lambda a: a.reshape(a.shape[1:])
    sq1 = lambda a: a.reshape(a.shape[:1] + a.shape[2:])
    (w_mod, w_in, conv_w, lru_wr, lru_wi, lru_br, lru_bi, lru_lam, w_a_out, w_b_out, w_o, w_router, w_e_gate, w_e_up,
     w_e_down) = map(sq0, (w_mod, w_in, conv_w, lru_wr, lru_wi, lru_br, lru_bi, lru_lam, w_a_out, w_b_out, w_o,
                           w_router, w_e_gate, w_e_up, w_e_down))
    state_rnn, cache_k, cache_v = map(sq1, (state_rnn, cache_k, cache_v))

    cvec = jnp.concatenate([c_ctx[None, :], c, jnp.zeros((SUBLANES - 1 - nbl, D_MODEL), F32)], axis=0)
    mod = _modulation(cvec, w_mod, b_mod)
    mod3 = mod.reshape(SUBLANES, 1, 6 * D_MODEL)

    proj = _in_proj(x_c, x_l, mod3, g_pre1, w_in.astype(BF16))

    rnn_w = (conv_w, conv_b, lru_wr, lru_wi, lru_br, lru_bi, lru_lam)
    ya_c, h_fin = _rnn_branch(proj, *rnn_w, jnp.zeros((nbc, 2, D_RNN), F32), seq=nc, nseq=SUBLANES,
                              nb=nbc // SUBLANES, row0=0)
    ya_l, _ = _rnn_branch(proj, *rnn_w, state_rnn, seq=nl, nseq=1, nb=nbl, row0=tc // nl)

    lam_vecs = (lam_q1, lam_k1, lam_q2, lam_k2)
    on_c = _attention(proj, lam_vecs, g_sub, n=nc, nb=nbc, row0=0, tq=nc)
    past = cache_k.shape[1]
    on_l = _attention(proj, lam_vecs, g_sub, n=nl, nb=nbl, row0=tc // nl, tq=512,
                      cache=(cache_k.reshape(nbl, past, N_HEADS * VD), cache_v.reshape(nbl, past, N_HEADS * VD)),
                      tables=_rope_tables(nl))

    x1, h2, aff_t = _post_mix(ya_c, ya_l, on_c, on_l, proj, x_c, x_l, mod3, g_post1, g_pre2, w_a_out.astype(BF16),
                              w_b_out.astype(BF16), w_o.astype(BF16), w_router.T)

    capc = EC_FACTOR * nc // N_EXPERTS
    idx_c, val_c, growt_c = _route(aff_t, n=nc, nsets=nbc, col0=0, row_stride=capc)
    idx_l, val_l, growt_l = _route(aff_t, n=nl, nsets=nbl, col0=tc // nl, row_stride=0)
    gid_c = idx_c[..., 0] + (jnp.arange(nbc, dtype=jnp.int32) * nc)[:, None, None]
    gid_l = idx_l[..., 0] + (tc + jnp.arange(nbl, dtype=jnp.int32) * nl)[:, None, None]
    gidx = jnp.concatenate([gid_c.transpose(1, 0, 2).reshape(N_EXPERTS, -1),
                            gid_l.transpose(1, 0, 2).reshape(N_EXPERTS, -1)], axis=1)
    vals = jnp.concatenate([val_c.transpose(1, 0, 2, 3).reshape(N_EXPERTS, -1, 1),
                            val_l.transpose(1, 0, 2, 3).reshape(N_EXPERTS, -1, 1)], axis=1)
    growt = jnp.concatenate([growt_c, growt_l], axis=0)

    ye = _experts(gidx, h2, vals, w_e_gate, w_e_up, w_e_down)
    y_c, y_l = _combine(growt, ye, x1, mod3, g_post2, tc)

    y_prompt = y_c.reshape(nbc, nc, D_MODEL)
    y_sample = y_l.reshape(nbl, nl, D_MODEL)
    kblk = (2 * D_RNN + N_HEADS * VD) // (N_HEADS * VD)
    new_k = _take_cols(proj, tc, kblk, N_HEADS * VD).reshape(nbc, 1, nc, N_HEADS, 2 * HD)
    new_v = _take_cols(proj, tc, kblk + 1, N_HEADS * VD).reshape(nbc, 1, nc, N_HEADS, VD)
    return (y_prompt, y_sample, new_k, new_v, h_fin[:, None])
```

```python
import functools
import math

import jax
import jax.numpy as jnp
from jax import lax
from jax.experimental import pallas as pl
from jax.experimental.pallas import tpu as pltpu

F32 = jnp.float32
BF16 = jnp.bfloat16
HIGHEST = lax.Precision.HIGHEST

D_MODEL = 2048
D_RNN = 1024
RNN_BLOCKS = 8
RNN_BS = 128
CONV_W = 4
LRU_C = 8.0
N_HEADS = 8
HD = 64
VD = 128
N_EXPERTS = 16
D_EXPERT = 1024
EC_FACTOR = 2
EPS = 1e-6
IN_COLS = 9216
GRID_W = 64
ROPE_BASE = 10000.0
LAM_INIT = 0.8 - 0.6 * math.exp(-0.3 * 0)

GROUP_TOKENS = 4096
GROUP_ROWS = 512
SUBLANES = 8
LANES = 128
V7X_VMEM_BYTES = 64 * 1024 * 1024


def _cparams(semantics, vmem_mb):
    return pltpu.CompilerParams(dimension_semantics=semantics, vmem_limit_bytes=vmem_mb * 1024 * 1024)


def _sigmoid(x):
    return 0.5 * jnp.tanh(0.5 * x) + 0.5


def _rms(x, g):
    return x * lax.rsqrt(jnp.mean(x * x, axis=-1, keepdims=True) + EPS) * g


def _mod_kernel(c_ref, w_ref, b_ref, o_ref):
    c = c_ref[...]
    s = c * _sigmoid(c)
    o_ref[...] = jnp.dot(s, w_ref[...], precision=HIGHEST, preferred_element_type=F32) + b_ref[...]


def _modulation(cvec, w_mod, b_mod):
    tn = 1024
    return pl.pallas_call(
        _mod_kernel,
        grid=(6 * D_MODEL // tn,),
        in_specs=[
            pl.BlockSpec((SUBLANES, D_MODEL), lambda j: (0, 0)),
            pl.BlockSpec((D_MODEL, tn), lambda j: (0, j)),
            pl.BlockSpec((1, tn), lambda j: (0, j)),
        ],
        out_specs=pl.BlockSpec((SUBLANES, tn), lambda j: (0, j)),
        out_shape=jax.ShapeDtypeStruct((SUBLANES, 6 * D_MODEL), F32),
        compiler_params=_cparams(("arbitrary",), 40),
        name="modulation",
    )(cvec, w_mod, b_mod)


def _in_kernel(xc_ref, xl_ref, sh_ref, sc_ref, g_ref, w_ref, o_ref, h_ref, *, ctx_tiles):
    i, j = pl.program_id(0), pl.program_id(1)
    rows = 128

    def prologue(x_ref):
        def body(c, carry):
            r0 = pl.multiple_of(c * rows, rows)
            h = _rms(x_ref[pl.ds(r0, rows), :], g_ref[...]) * (1.0 + sc_ref[...]) + sh_ref[...]
            h_ref[pl.ds(r0, rows), :] = h.astype(BF16)
            return carry

        lax.fori_loop(0, h_ref.shape[0] // rows, body, 0)

    @pl.when((j == 0) & (i < ctx_tiles))
    def _():
        prologue(xc_ref)

    @pl.when((j == 0) & (i >= ctx_tiles))
    def _():
        prologue(xl_ref)

    o_ref[...] = jnp.dot(h_ref[...], w_ref[...], preferred_element_type=F32)


def _in_proj(x_c, x_l, mod3, g_pre1, w_in_bf):
    T = x_c.shape[0] + x_l.shape[0]
    tm, tn = 1024, 1024
    tpg = GROUP_TOKENS // tm
    nct = x_c.shape[0] // tm
    return pl.pallas_call(
        functools.partial(_in_kernel, ctx_tiles=nct),
        grid=(T // tm, IN_COLS // tn),
        in_specs=[
            pl.BlockSpec((tm, D_MODEL), lambda i, j: (jnp.minimum(i, nct - 1), 0), pipeline_mode=pl.Buffered(1)),
            pl.BlockSpec((tm, D_MODEL), lambda i, j: (jnp.maximum(i - nct, 0), 0)),
            pl.BlockSpec((None, 1, D_MODEL), lambda i, j: (i // tpg, 0, 0)),
            pl.BlockSpec((None, 1, D_MODEL), lambda i, j: (i // tpg, 0, 1)),
            pl.BlockSpec((1, D_MODEL), lambda i, j: (0, 0)),
            pl.BlockSpec((D_MODEL, tn), lambda i, j: (0, j)),
        ],
        out_specs=pl.BlockSpec((tm, tn), lambda i, j: (i, j)),
        out_shape=jax.ShapeDtypeStruct((T, IN_COLS), F32),
        scratch_shapes=[pltpu.VMEM((tm, D_MODEL), BF16)],
        compiler_params=_cparams(("arbitrary", "arbitrary"), 56),
        name="in_proj",
    )(x_c, x_l, mod3, mod3, g_pre1, w_in_bf)


def _copy_kernel(x_ref, o_ref):
    o_ref[...] = x_ref[...]


def _take_cols(proj, rows, col_block, width):
    tm = 1024
    return pl.pallas_call(
        _copy_kernel,
        grid=(rows // tm,),
        in_specs=[pl.BlockSpec((tm, width), lambda i: (i, col_block))],
        out_specs=pl.BlockSpec((tm, width), lambda i: (i, 0)),
        out_shape=jax.ShapeDtypeStruct((rows, width), proj.dtype),
        compiler_params=_cparams(("arbitrary",), 40),
        name="take_cols",
    )(proj)


def _gelu_tanh(x):
    return 0.5 * x * (1.0 + jnp.tanh(math.sqrt(2.0 / math.pi) * (x + 0.044715 * (x * x * x))))


def _softplus(x):
    return jnp.maximum(x, 0.0) + jnp.log(1.0 + jnp.exp(-jnp.abs(x)))


def _split_bf16(x):
    hi = x.astype(BF16)
    return hi, (x - hi.astype(F32)).astype(BF16)


def _dot_3pass(x, w_hi, w_lo):
    x_hi, x_lo = _split_bf16(x)
    d = functools.partial(jnp.dot, preferred_element_type=F32)
    return d(x_hi, w_hi) + (d(x_lo, w_hi) + d(x_hi, w_lo))


def _rnn_kernel(xr_ref, gr_ref, cw_ref, cb_ref, wr_ref, wi_ref, br_ref, bi_ref, lam_ref, h0_ref,
                ya_ref, hf_ref, xp_ref, a_ref, b_ref, h_ref, p_ref, *, seq, nseq):
    n = seq * nseq
    seg = n // SUBLANES
    pitch = seg + SUBLANES
    zeros8 = jnp.zeros((SUBLANES, LANES), F32)
    xp_ref[0:SUBLANES, :] = zeros8
    xp_ref[SUBLANES + n:2 * SUBLANES + n, :] = zeros8
    xp_ref[SUBLANES:SUBLANES + n, :] = xr_ref[...]
    cw = cw_ref[...]
    cb = cb_ref[...]
    sp = _softplus(-lam_ref[...])
    w_r = [_split_bf16(wr_ref[d]) for d in range(2)]
    w_i = [_split_bf16(wi_ref[d]) for d in range(2)]

    def gates(s, carry):
        r0 = pl.multiple_of(s * seg, SUBLANES)
        taps = [xp_ref[pl.ds(r0 + 6 + k, seg), :] for k in range(CONV_W)]
        if nseq > 1:
            t = (r0 + lax.broadcasted_iota(jnp.int32, (seg, LANES), 0)) % seq
            taps[0] = jnp.where(t >= 2, taps[0], 0.0)
            taps[1] = jnp.where(t >= 1, taps[1], 0.0)
            taps[3] = jnp.where(t <= seq - 2, taps[3], 0.0)
        xc = cb + taps[0] * cw[0:1]
        xc = xc + taps[1] * cw[1:2]
        xc = xc + taps[2] * cw[2:3]
        xc = xc + taps[3] * cw[3:4]
        p0 = pl.multiple_of(s * pitch, SUBLANES)
        for d in range(2):
            r = _sigmoid(_dot_3pass(xc, *w_r[d]) + br_ref[d:d + 1, :])
            i = _sigmoid(_dot_3pass(xc, *w_i[d]) + bi_ref[d:d + 1, :])
            a = jnp.exp((-LRU_C) * r * sp[d:d + 1, :])
            y = 1.0 - a * a
            a_ref[d, pl.ds(p0, seg), :] = a
            b_ref[d, pl.ds(p0, seg), :] = jnp.where(y > 0.0, y * lax.rsqrt(y), 0.0) * (i * xc)
        return carry

    lax.fori_loop(0, SUBLANES, gates, 0)

    def scan(j, carry):
        hf, af, hb, ab = carry
        a0 = a_ref[0, pl.ds(j, SUBLANES, stride=pitch), :]
        b0 = b_ref[0, pl.ds(j, SUBLANES, stride=pitch), :]
        hf = a0 * hf + b0
        af = a0 * af
        h_ref[0, pl.ds(j, SUBLANES, stride=pitch), :] = hf
        p_ref[0, pl.ds(j, SUBLANES, stride=pitch), :] = af
        jb = seg - 1 - j
        a1 = a_ref[1, pl.ds(jb, SUBLANES, stride=pitch), :]
        b1 = b_ref[1, pl.ds(jb, SUBLANES, stride=pitch), :]
        hb = a1 * hb + b1
        ab = a1 * ab
        h_ref[1, pl.ds(jb, SUBLANES, stride=pitch), :] = hb
        p_ref[1, pl.ds(jb, SUBLANES, stride=pitch), :] = ab
        return hf, af, hb, ab

    ones8 = jnp.ones((SUBLANES, LANES), F32)
    ef, pf, eb, pb = lax.fori_loop(0, seg, scan, (zeros8, ones8, zeros8, ones8), unroll=8)

    spq = SUBLANES // nseq
    cf = [None] * SUBLANES
    cbk = [None] * SUBLANES
    for q in range(nseq):
        h0 = h0_ref[q]
        first, last = q * spq, q * spq + spq - 1
        cf[first] = h0[0:1, :]
        for s in range(first + 1, last + 1):
            cf[s] = ef[s - 1:s, :] + pf[s - 1:s, :] * cf[s - 1]
        cbk[last] = h0[1:2, :]
        for s in range(last - 1, first - 1, -1):
            cbk[s] = eb[s + 1:s + 2, :] + pb[s + 1:s + 2, :] * cbk[s + 1]
        hf_ref[q, 0:1, :] = ef[last:last + 1, :] + pf[last:last + 1, :] * cf[last]
        hf_ref[q, 1:2, :] = eb[first:first + 1, :] + pb[first:first + 1, :] * cbk[first]

    for s in range(SUBLANES):
        hfw = h_ref[0, s * pitch:s * pitch + seg, :] + p_ref[0, s * pitch:s * pitch + seg, :] * cf[s]
        hbw = h_ref[1, s * pitch:s * pitch + seg, :] + p_ref[1, s * pitch:s * pitch + seg, :] * cbk[s]
        y = (hfw + hbw) * _gelu_tanh(gr_ref[s * seg:(s + 1) * seg, :])
        ya_ref[s * seg:(s + 1) * seg, :] = y.astype(ya_ref.dtype)


def _rnn_branch(proj, conv_w, conv_b, wr, wi, br, bi, lam, h0, *, seq, nseq, nb, row0):
    n = seq * nseq
    seg = n // SUBLANES
    pitch = seg + SUBLANES
    return pl.pallas_call(
        functools.partial(_rnn_kernel, seq=seq, nseq=nseq),
        grid=(nb, RNN_BLOCKS),
        in_specs=[
            pl.BlockSpec((n, RNN_BS), lambda b, k: (row0 + b, k)),
            pl.BlockSpec((n, RNN_BS), lambda b, k: (row0 + b, RNN_BLOCKS + k)),
            pl.BlockSpec((CONV_W, RNN_BS), lambda b, k: (0, k)),
            pl.BlockSpec((1, RNN_BS), lambda b, k: (0, k)),
            pl.BlockSpec((2, None, RNN_BS, RNN_BS), lambda b, k: (0, k, 0, 0)),
            pl.BlockSpec((2, None, RNN_BS, RNN_BS), lambda b, k: (0, k, 0, 0)),
            pl.BlockSpec((2, RNN_BS), lambda b, k: (0, k)),
            pl.BlockSpec((2, RNN_BS), lambda b, k: (0, k)),
            pl.BlockSpec((2, RNN_BS), lambda b, k: (0, k)),
            pl.BlockSpec((nseq, 2, RNN_BS), lambda b, k: (b, 0, k)),
        ],
        out_specs=[
            pl.BlockSpec((n, RNN_BS), lambda b, k: (b, k)),
            pl.BlockSpec((nseq, 2, RNN_BS), lambda b, k: (b, 0, k)),
        ],
        out_shape=[jax.ShapeDtypeStruct((nb * n, D_RNN), BF16), jax.ShapeDtypeStruct((nb * nseq, 2, D_RNN), F32)],
        scratch_shapes=[
            pltpu.VMEM((n + 2 * SUBLANES, RNN_BS), F32),
        ] + [pltpu.VMEM((2, SUBLANES * pitch, RNN_BS), F32)] * 4,
        compiler_params=_cparams(("arbitrary", "arbitrary"), 40),
        name=f"rnn_n{n}",
    )(proj, proj, conv_w, conv_b, wr, wi, br, bi, lam, h0)


def _rope(x, cos, sin_signed):
    lane = lax.broadcasted_iota(jnp.int32, x.shape, 1)
    partner = jnp.where((lane % HD) < HD // 2, pltpu.roll(x, LANES - HD // 2, 1), pltpu.roll(x, HD // 2, 1))
    return x * cos + partner * sin_signed


ATTN_SUB = 128
ATTN_LOOKAHEAD = 4


def _attn_kernel(*refs, n, npast, tq, rope):
    if rope:
        (lq1, lk1, lq2, lk2, q_ref, k_ref, v_ref, gsub_ref, ck_ref, cv_ref, cosk_ref, sink_ref, cosq_ref, sinq_ref,
         o_ref, kb_ref, vb_ref, s_ref, p_ref) = refs
    else:
        lq1, lk1, lq2, lk2, q_ref, k_ref, v_ref, gsub_ref, o_ref, kb_ref, vb_ref, s_ref, p_ref = refs
    chunk = 256
    nk = n + npast

    @pl.when(pl.program_id(2) == 0)
    def _():
        ones_col = jnp.where(lax.broadcasted_iota(jnp.int32, (chunk, VD), 1) == 0, 1.0, 0.0).astype(BF16)
        if npast:
            for r0 in range(0, npast, chunk):
                kb_ref[:, r0:r0 + chunk] = ck_ref[r0:r0 + chunk, :].T.astype(BF16)
                vb_ref[r0:r0 + chunk, VD:2 * VD] = ones_col
            vb_ref[0:npast, 0:VD] = cv_ref[...].astype(BF16)

        for r0 in range(0, n, chunk):
            k = k_ref[r0:r0 + chunk, :]
            if rope:
                k = _rope(k, cosk_ref[r0:r0 + chunk, :], sink_ref[r0:r0 + chunk, :])
            kb_ref[:, npast + r0:npast + r0 + chunk] = k.T.astype(BF16)
            vb_ref[npast + r0:npast + r0 + chunk, 0:VD] = v_ref[r0:r0 + chunk, :].astype(BF16)
            vb_ref[npast + r0:npast + r0 + chunk, VD:2 * VD] = ones_col

    lam = (jnp.exp(jnp.sum(lq1[...] * lk1[...], axis=-1, keepdims=True))
           - jnp.exp(jnp.sum(lq2[...] * lk2[...], axis=-1, keepdims=True)) + LAM_INIT)
    q = q_ref[...]
    if rope:
        q = _rope(q, cosq_ref[...], sinq_ref[...])
    q = q * (HD ** -0.5 * math.log2(math.e))
    lane = lax.broadcasted_iota(jnp.int32, (ATTN_SUB, VD), 1)
    nt = (((1,), (1,)), ((), ()))
    kc = 2 * LANES
    key_chunks = [(c0, min(kc, nk - c0)) for c0 in range(0, nk, kc)]
    nunits = 2 * (tq // ATTN_SUB)

    def scores(unit):
        sub, comp = divmod(unit, 2)
        qs = q[sub * ATTN_SUB:(sub + 1) * ATTN_SUB, :]
        qc = jnp.where((lane < HD) == (comp == 0), qs, 0.0).astype(BF16)
        m_run = None
        for c0, w in key_chunks:
            s = jnp.dot(qc, kb_ref[:, c0:c0 + w], preferred_element_type=F32)
            s_ref[unit, :, c0:c0 + w] = s
            for l0 in range(0, w, LANES):
                piece = s[:, l0:l0 + LANES]
                m_run = piece if m_run is None else jnp.maximum(m_run, piece)
        return jnp.max(m_run, axis=-1, keepdims=True)

    def values(unit, m):
        for c0, w in key_chunks:
            p_ref[unit, :, c0:c0 + w] = jnp.exp2(s_ref[unit, :, c0:c0 + w] - m).astype(BF16)
        return jnp.dot(p_ref[unit], vb_ref[...], preferred_element_type=F32)

    outs = [None] * nunits
    ahead = ATTN_LOOKAHEAD
    ms = [scores(u) for u in range(min(ahead, nunits))]
    for unit in range(nunits):
        if unit + ahead < nunits:
            ms.append(scores(unit + ahead))
        outs[unit] = values(unit, ms[unit])
        if unit % 2 == 1:
            sub = unit // 2
            o1, o2 = outs[unit - 1], outs[unit]
            o = o1[:, 0:VD] * (1.0 / o1[:, VD:VD + 1]) - o2[:, 0:VD] * (lam / o2[:, VD:VD + 1])
            o = _rms(o, gsub_ref[...]) * (1.0 - LAM_INIT)
            o_ref[sub * ATTN_SUB:(sub + 1) * ATTN_SUB, :] = o.astype(o_ref.dtype)


def _attention(proj, lam_vecs, g_sub, *, n, nb, row0, tq, cache=None, tables=None):
    rope = tables is not None
    npast = 0 if cache is None else cache[0].shape[1]
    nk = n + npast
    qpr = n // tq
    qcol, kcol, vcol = 2 * RNN_BLOCKS, 2 * RNN_BLOCKS + N_HEADS, 2 * RNN_BLOCKS + 2 * N_HEADS
    vec = pl.BlockSpec((1, HD), lambda b, h, t: (0, 0))
    in_specs = [vec, vec, vec, vec,
                pl.BlockSpec((tq, VD), lambda b, h, t: ((row0 + b) * qpr + t, qcol + h)),
                pl.BlockSpec((n, VD), lambda b, h, t: (row0 + b, kcol + h)),
                pl.BlockSpec((n, VD), lambda b, h, t: (row0 + b, vcol + h)),
                pl.BlockSpec((1, VD), lambda b, h, t: (0, 0))]
    args = [*lam_vecs, proj, proj, proj, g_sub]
    if rope:
        ck, cv = cache
        cos_t, sin_t = tables
        in_specs += [pl.BlockSpec((None, npast, VD), lambda b, h, t: (b, 0, h)),
                     pl.BlockSpec((None, npast, VD), lambda b, h, t: (b, 0, h)),
                     pl.BlockSpec((n, VD), lambda b, h, t: (0, 0), pipeline_mode=pl.Buffered(1)),
                     pl.BlockSpec((n, VD), lambda b, h, t: (0, 0), pipeline_mode=pl.Buffered(1)),
                     pl.BlockSpec((tq, VD), lambda b, h, t: (t, 0)),
                     pl.BlockSpec((tq, VD), lambda b, h, t: (t, 0))]
        args += [ck, cv, cos_t, sin_t, cos_t, sin_t]
    return pl.pallas_call(
        functools.partial(_attn_kernel, n=n, npast=npast, tq=tq, rope=rope),
        grid=(nb, N_HEADS, qpr),
        in_specs=in_specs,
        out_specs=pl.BlockSpec((tq, VD), lambda b, h, t: (b * qpr + t, h)),
        out_shape=jax.ShapeDtypeStruct((nb * n, N_HEADS * VD), BF16),
        scratch_shapes=[pltpu.VMEM((VD, nk), BF16), pltpu.VMEM((nk, 2 * VD), BF16),
                        pltpu.VMEM((2 * tq // ATTN_SUB, ATTN_SUB, nk), F32),
                        pltpu.VMEM((2 * tq // ATTN_SUB, ATTN_SUB, nk), BF16)],
        compiler_params=_cparams(("arbitrary", "arbitrary", "arbitrary"), 56),
        name=f"attn_n{n}",
    )(*args)


POST_TM = 512
POST_UNIT = 256
OUT_UNIT = 128


def _merge_kernel(yac_ref, yal_ref, onc_ref, onl_ref, ga0_ref, ga1_ref, gb0_ref, gb1_ref, wa_ref, wb_ref, m_ref,
                  *, ctx_tiles):
    is_ctx = pl.program_id(0) < ctx_tiles
    half = D_MODEL // 2
    for r0 in range(0, POST_TM, POST_UNIT):
        rs = slice(r0, r0 + POST_UNIT)
        ya = jnp.where(is_ctx, yac_ref[rs, :], yal_ref[rs, :])
        on = jnp.where(is_ctx, onc_ref[rs, :], onl_ref[rs, :])
        oa = jnp.dot(ya, wa_ref[...], preferred_element_type=F32)
        ob = jnp.dot(on, wb_ref[...], preferred_element_type=F32)
        m_ref[rs, 0:half] = (_sigmoid(ga0_ref[rs, :]) * oa[:, 0:half]
                             + _sigmoid(gb0_ref[rs, :]) * ob[:, 0:half]).astype(BF16)
        m_ref[rs, half:] = (_sigmoid(ga1_ref[rs, :]) * oa[:, half:]
                            + _sigmoid(gb1_ref[rs, :]) * ob[:, half:]).astype(BF16)


def _outproj_kernel(m_ref, xc_ref, xl_ref, g1_ref, sh2_ref, sc2_ref, gpost1_ref, gpre2_ref, wo_ref, wrt_ref,
                    x1_ref, h2_ref, aff_ref, *, ctx_tiles):
    is_ctx = pl.program_id(0) < ctx_tiles
    nt = functools.partial(lax.dot_general, dimension_numbers=(((1,), (1,)), ((), ())), preferred_element_type=F32)
    w_hi, w_lo = _split_bf16(wrt_ref[...])
    units = [slice(r0, r0 + OUT_UNIT) for r0 in range(0, POST_TM, OUT_UNIT)]
    ts = [jnp.dot(m_ref[rs, :], wo_ref[...], preferred_element_type=F32) for rs in units]
    for rs, t in zip(units, ts):
        x1 = jnp.where(is_ctx, xc_ref[rs, :], xl_ref[rs, :]) + g1_ref[...] * _rms(t, gpost1_ref[...])
        x1_ref[rs, :] = x1
        h2 = _rms(x1, gpre2_ref[...]) * (1.0 + sc2_ref[...]) + sh2_ref[...]
        h2_ref[rs, :] = h2
        h_hi, h_lo = _split_bf16(h2)
        logits = nt(w_hi, h_hi) + (nt(w_lo, h_hi) + nt(w_hi, h_lo))
        e = jnp.exp(logits - jnp.max(logits, axis=0, keepdims=True))
        aff_ref[:, rs] = e / jnp.sum(e, axis=0, keepdims=True)


def _post_mix(ya_c, ya_l, on_c, on_l, proj, x_c, x_l, mod3, g_post1, g_pre2, wa, wb, wo, wrt):
    T = x_c.shape[0] + x_l.shape[0]
    tm = POST_TM
    tpg = GROUP_TOKENS // tm
    nct = ya_c.shape[0] // tm
    ctx_blk = lambda w: pl.BlockSpec((tm, w), lambda i: (jnp.minimum(i, nct - 1), 0), pipeline_mode=pl.Buffered(1))
    lat_blk = lambda w: pl.BlockSpec((tm, w), lambda i: (jnp.maximum(i - nct, 0), 0))
    half = D_MODEL // 2
    gate0 = (2 * D_RNN + 2 * N_HEADS * VD + N_HEADS * VD) // half
    row = lambda c: pl.BlockSpec((None, 1, D_MODEL), lambda i: (i // tpg, 0, c))
    const = lambda shape: pl.BlockSpec(shape, lambda i: (0,) * len(shape), pipeline_mode=pl.Buffered(1))
    merged = pl.pallas_call(
        functools.partial(_merge_kernel, ctx_tiles=nct),
        grid=(T // tm,),
        in_specs=[
            ctx_blk(D_RNN), lat_blk(D_RNN), ctx_blk(N_HEADS * VD), lat_blk(N_HEADS * VD),
            pl.BlockSpec((tm, half), lambda i: (i, gate0)),
            pl.BlockSpec((tm, half), lambda i: (i, gate0 + 1)),
            pl.BlockSpec((tm, half), lambda i: (i, gate0 + 2)),
            pl.BlockSpec((tm, half), lambda i: (i, gate0 + 3)),
            const((D_RNN, D_MODEL)), const((N_HEADS * VD, D_MODEL)),
        ],
        out_specs=pl.BlockSpec((tm, D_MODEL), lambda i: (i, 0)),
        out_shape=jax.ShapeDtypeStruct((T, D_MODEL), BF16),
        compiler_params=_cparams(("arbitrary",), 56),
        name="merge",
    )(ya_c, ya_l, on_c, on_l, proj, proj, proj, proj, wa, wb)
    return pl.pallas_call(
        functools.partial(_outproj_kernel, ctx_tiles=nct),
        grid=(T // tm,),
        in_specs=[
            pl.BlockSpec((tm, D_MODEL), lambda i: (i, 0)),
            ctx_blk(D_MODEL), lat_blk(D_MODEL),
            row(2), row(3), row(4),
            const((1, D_MODEL)), const((1, D_MODEL)),
            const((D_MODEL, D_MODEL)),
            const((N_EXPERTS, D_MODEL)),
        ],
        out_specs=[
            pl.BlockSpec((tm, D_MODEL), lambda i: (i, 0)),
            pl.BlockSpec((tm, D_MODEL), lambda i: (i, 0)),
            pl.BlockSpec((N_EXPERTS, tm), lambda i: (0, i)),
        ],
        out_shape=[jax.ShapeDtypeStruct((T, D_MODEL), F32), jax.ShapeDtypeStruct((T, D_MODEL), F32),
                   jax.ShapeDtypeStruct((N_EXPERTS, T), F32)],
        compiler_params=_cparams(("arbitrary",), 56),
        name="out_proj",
    )(merged, x_c, x_l, mod3, mod3, mod3, g_post1, g_pre2, wo, wrt)


def _cumsum_excl(x, tri):
    n = x.shape[1]
    blk = tri.shape[0]
    outs = []
    carry = jnp.zeros((x.shape[0], 1), F32)
    for c in range(n // blk):
        xc = x[:, c * blk:(c + 1) * blk]
        outs.append(jnp.dot(xc.astype(BF16), tri, preferred_element_type=F32) + carry)
        carry = carry + jnp.sum(xc, axis=1, keepdims=True)
    return outs[0] if len(outs) == 1 else jnp.concatenate(outs, axis=1)


def _route_kernel(aff_ref, idx_ref, val_ref, growt_ref, *, n, cap, row_stride):
    aff = aff_ref[...]

    def bisect(_, lohi):
        lo, hi = lohi
        mid = lo + ((hi - lo) >> 1)
        cnt = jnp.sum((aff >= pltpu.bitcast(mid, F32)).astype(jnp.int32), axis=1, keepdims=True)
        ok = cnt >= cap
        return jnp.where(ok, mid, lo), jnp.where(ok, hi, mid)

    lo0 = jnp.zeros((N_EXPERTS, 1), jnp.int32)
    hi0 = jnp.full((N_EXPERTS, 1), 0x7F800000, jnp.int32)
    thr, _ = lax.fori_loop(0, 31, bisect, (lo0, hi0))
    gt = aff >= pltpu.bitcast(thr + 1, F32)
    eq = (aff >= pltpu.bitcast(thr, F32)) & jnp.logical_not(gt)
    need = cap - jnp.sum(gt.astype(jnp.int32), axis=1, keepdims=True)
    blk = min(n, 256)
    tri = (lax.broadcasted_iota(jnp.int32, (blk, blk), 0) < lax.broadcasted_iota(jnp.int32, (blk, blk), 1)).astype(BF16)
    eqpos = _cumsum_excl(eq.astype(F32), tri)
    sel = gt | (eq & (eqpos < need.astype(F32)))
    pos = _cumsum_excl(sel.astype(F32), tri).astype(jnp.int32)
    slot = jnp.where(sel, pos, -1)
    grow = jnp.where(sel, pos + pl.program_id(0) * row_stride, -1)
    growt_ref[...] = grow.astype(F32).T.astype(jnp.int32)

    rc = min(cap, 32)
    tok = lax.broadcasted_iota(jnp.int32, (rc, n), 1)
    for e in range(N_EXPERTS):
        slot_e = slot[e:e + 1, :]
        aff_e = aff[e:e + 1, :]

        def extract(c, carry):
            r0 = pl.multiple_of(c * rc, rc)
            hit = slot_e == (lax.broadcasted_iota(jnp.int32, (rc, n), 0) + r0)
            idx_ref[e, pl.ds(r0, rc), :] = jnp.sum(jnp.where(hit, tok, 0), axis=1, keepdims=True)
            val_ref[e, pl.ds(r0, rc), :] = jnp.sum(jnp.where(hit, aff_e, 0.0), axis=1, keepdims=True)
            return carry

        lax.fori_loop(0, cap // rc, extract, 0)


def _route(aff_t, *, n, nsets, col0, row_stride):
    cap = EC_FACTOR * n // N_EXPERTS
    kern = functools.partial(_route_kernel, n=n, cap=cap, row_stride=row_stride)
    return pl.pallas_call(
        kern,
        grid=(nsets,),
        in_specs=[pl.BlockSpec((N_EXPERTS, n), lambda s: (0, col0 + s))],
        out_specs=[
            pl.BlockSpec((None, N_EXPERTS, cap, 1), lambda s: (s, 0, 0, 0)),
            pl.BlockSpec((None, N_EXPERTS, cap, 1), lambda s: (s, 0, 0, 0)),
            pl.BlockSpec((n, N_EXPERTS), lambda s: (s, 0)),
        ],
        out_shape=[jax.ShapeDtypeStruct((nsets, N_EXPERTS, cap, 1), jnp.int32),
                   jax.ShapeDtypeStruct((nsets, N_EXPERTS, cap, 1), F32),
                   jax.ShapeDtypeStruct((nsets * n, N_EXPERTS), jnp.int32)],
        compiler_params=_cparams(("arbitrary",), 48),
        name=f"route_n{n}",
    )(aff_t)


WEIGHT_PARTS = 4


def _expert_kernel(cur_ref, nxt_ref, h2_hbm, val_ref, wg_hbm, wu_hbm, wd_hbm, ye_ref, xs0_ref, xs1_ref,
                   wg_ref, wu_ref, wd_ref, sg_ref, su_ref, sd_ref, sem, wsem, *, rows):
    e, grp = pl.program_id(0), pl.program_id(1)
    step = e * pl.num_programs(1) + grp
    nsteps = pl.num_programs(0) * pl.num_programs(1)
    bufs = (xs0_ref, xs1_ref)
    rin, rdn = D_MODEL // WEIGHT_PARTS, D_EXPERT // WEIGHT_PARTS

    def slab_copies(ex, part):
        r_in = pl.multiple_of(part * rin, rin)
        r_dn = pl.multiple_of(part * rdn, rdn)
        return (pltpu.make_async_copy(wg_hbm.at[ex, pl.ds(r_in, rin), :], sg_ref, wsem.at[0]),
                pltpu.make_async_copy(wu_hbm.at[ex, pl.ds(r_in, rin), :], su_ref, wsem.at[1]),
                pltpu.make_async_copy(wd_hbm.at[ex, pl.ds(r_dn, rdn), :], sd_ref, wsem.at[2]))

    def fetch(ex, part):
        for cp in slab_copies(ex, part):
            cp.start()

    def land(ex, part):
        for cp in slab_copies(ex, part):
            cp.wait()
        st = ex % 2
        wg_ref[st, pl.ds(pl.multiple_of(part * rin, rin), rin), :] = sg_ref[...].astype(BF16)
        wu_ref[st, pl.ds(pl.multiple_of(part * rin, rin), rin), :] = su_ref[...].astype(BF16)
        wd_ref[st, pl.ds(pl.multiple_of(part * rdn, rdn), rdn), :] = sd_ref[...].astype(BF16)

    @pl.when(step == 0)
    def _():
        for part in range(WEIGHT_PARTS):
            fetch(0, part)
            land(0, part)

    has_next = e + 1 < pl.num_programs(0)

    @pl.when(has_next & (grp >= 1) & (grp <= WEIGHT_PARTS))
    def _():
        land(e + 1, grp - 1)

    @pl.when(has_next & (grp < WEIGHT_PARTS))
    def _():
        fetch(e + 1, grp)

    wset = e % 2

    def row_copy(idx_ref, r, s):
        return pltpu.make_async_copy(h2_hbm.at[pl.ds(idx_ref[0, r], 1), :], bufs[s].at[pl.ds(r, 1), :], sem.at[s])

    def slot_wait(s):
        pltpu.make_async_copy(h2_hbm.at[pl.ds(0, rows), :], bufs[s], sem.at[s]).wait()

    @pl.when(step == 0)
    def _():
        def issue(r, carry):
            row_copy(cur_ref, r, 0).start()
            return carry

        lax.fori_loop(0, rows, issue, 0, unroll=8)

    def body(s):
        slot_wait(s)
        for r in range(rows):
            row_copy(nxt_ref, r, 1 - s).start()
        half = rows // 2
        hids = []
        for c in range(2):
            xs = bufs[s][c * half:(c + 1) * half, :].astype(BF16)
            g = jnp.dot(xs, wg_ref[wset], preferred_element_type=F32)
            u = jnp.dot(xs, wu_ref[wset], preferred_element_type=F32)
            hids.append(((g * _sigmoid(g)) * u).astype(BF16))
        for c in range(2):
            y = jnp.dot(hids[c], wd_ref[wset], preferred_element_type=F32) * val_ref[c * half:(c + 1) * half, :]
            ye_ref[c * half:(c + 1) * half, :] = y.astype(ye_ref.dtype)

        @pl.when(step == nsteps - 1)
        def _():
            slot_wait(1 - s)

    for s in range(2):
        pl.when(step % 2 == s)(functools.partial(body, s))


def _experts(gidx, h2, vals, wg, wu, wd):
    rows_per_expert = vals.shape[1]
    rows = GROUP_ROWS
    groups = rows_per_expert // rows
    assert groups > WEIGHT_PARTS
    last = N_EXPERTS * groups - 1
    hbm = pl.BlockSpec(memory_space=pl.ANY)
    return pl.pallas_call(
        functools.partial(_expert_kernel, rows=rows),
        grid=(N_EXPERTS, groups),
        in_specs=[
            pl.BlockSpec((None, 1, rows), lambda e, g: (e * groups + g, 0, 0), memory_space=pltpu.SMEM),
            pl.BlockSpec((None, 1, rows), lambda e, g: (jnp.minimum(e * groups + g + 1, last), 0, 0),
                         memory_space=pltpu.SMEM),
            hbm,
            pl.BlockSpec((None, rows, 1), lambda e, g: (e, g, 0)),
            hbm, hbm, hbm,
        ],
        out_specs=pl.BlockSpec((None, rows, D_MODEL), lambda e, g: (e, g, 0)),
        out_shape=jax.ShapeDtypeStruct((N_EXPERTS, rows_per_expert, D_MODEL), BF16),
        scratch_shapes=[pltpu.VMEM((rows, D_MODEL), F32), pltpu.VMEM((rows, D_MODEL), F32),
                        pltpu.VMEM((2, D_MODEL, D_EXPERT), BF16), pltpu.VMEM((2, D_MODEL, D_EXPERT), BF16),
                        pltpu.VMEM((2, D_EXPERT, D_MODEL), BF16),
                        pltpu.VMEM((D_MODEL // WEIGHT_PARTS, D_EXPERT), F32),
                        pltpu.VMEM((D_MODEL // WEIGHT_PARTS, D_EXPERT), F32),
                        pltpu.VMEM((D_EXPERT // WEIGHT_PARTS, D_MODEL), F32),
                        pltpu.SemaphoreType.DMA((2,)), pltpu.SemaphoreType.DMA((3,))],
        compiler_params=_cparams(("arbitrary", "arbitrary"), 56),
        name="experts",
    )(gidx.reshape(N_EXPERTS * groups, 1, rows), gidx.reshape(N_EXPERTS * groups, 1, rows), h2, vals, wg, wu, wd)


COMBINE_TM = 256
COMBINE_WIN = 128
BF16_ROW_TILE = 16


def _combine_kernel(win_ref, fast_ref, bounds_ref, growt_ref, ye_hbm, x1_ref, g2_ref, gpost2_ref, oc_ref, ol_ref,
                    wbuf_ref, oh_ref, acc_ref, sbuf_ref, wsem, ssem, *, ctx_steps):
    i = pl.program_id(0)
    nsteps = pl.num_programs(0)
    tpg = GROUP_TOKENS // COMBINE_TM
    slot = i % 2
    half = GROUP_ROWS // 2
    growt = growt_ref[...]

    def window_copy(step, e, s):
        start = pl.multiple_of((step // tpg) * GROUP_ROWS + win_ref[step * N_EXPERTS + e], BF16_ROW_TILE)
        return pltpu.make_async_copy(ye_hbm.at[e, pl.ds(start, COMBINE_WIN), :],
                                     wbuf_ref.at[s, pl.ds(e * COMBINE_WIN, COMBINE_WIN), :], wsem.at[s])

    def fetch(step, s):
        for e in range(N_EXPERTS):
            window_copy(step, e, s).start()

    @pl.when((i == 0) & (fast_ref[0] == 1))
    def _():
        fetch(0, 0)

    nxt = jnp.minimum(i + 1, nsteps - 1)

    @pl.when((i + 1 < nsteps) & (fast_ref[nxt] == 1))
    def _():
        fetch(nxt, 1 - slot)

    @pl.when(fast_ref[i] == 1)
    def _():
        for e in range(N_EXPERTS):
            window_copy(i, e, slot).wait()
        for e in range(N_EXPERTS):
            rows = win_ref[i * N_EXPERTS + e] + lax.broadcasted_iota(jnp.int32, (COMBINE_TM, COMBINE_WIN), 1)
            oh_ref[:, e * COMBINE_WIN:(e + 1) * COMBINE_WIN] = jnp.where(growt[:, e:e + 1] == rows, 1.0, 0.0).astype(BF16)
        acc_ref[...] = jnp.dot(oh_ref[...], wbuf_ref[slot], preferred_element_type=F32)

    @pl.when(fast_ref[i] == 0)
    def _():
        acc_ref[...] = jnp.zeros_like(acc_ref)
        for e in range(N_EXPERTS):
            lo = bounds_ref[(i * N_EXPERTS + e) * 2]
            hi = bounds_ref[(i * N_EXPERTS + e) * 2 + 1]
            for c in range(2):
                @pl.when((lo < (c + 1) * half) & (hi > c * half))
                def _():
                    cp = pltpu.make_async_copy(ye_hbm.at[e, pl.ds((i // tpg) * GROUP_ROWS + c * half, half), :],
                                               sbuf_ref, ssem)
                    cp.start()
                    cp.wait()
                    rows = lax.broadcasted_iota(jnp.int32, (COMBINE_TM, half), 1) + c * half
                    onehot = jnp.where(growt[:, e:e + 1] == rows, 1.0, 0.0).astype(BF16)
                    acc_ref[...] += jnp.dot(onehot, sbuf_ref[...], preferred_element_type=F32)

    y = x1_ref[...] + g2_ref[...] * _rms(acc_ref[...], gpost2_ref[...])

    @pl.when(i < ctx_steps)
    def _():
        oc_ref[...] = y

    @pl.when(i >= ctx_steps)
    def _():
        ol_ref[...] = y


def _combine(growt, ye, x1, mod3, g_post2, ctx_tokens):
    T = x1.shape[0]
    tm = COMBINE_TM
    tpg = GROUP_TOKENS // tm
    g = growt.reshape(T // tm, tm, N_EXPERTS)
    lo = jnp.min(jnp.where(g >= 0, g, GROUP_ROWS), axis=1)
    hi = jnp.max(g, axis=1) + 1
    win = jnp.clip((lo // BF16_ROW_TILE) * BF16_ROW_TILE, 0, GROUP_ROWS - COMBINE_WIN)
    fast = jnp.all((hi <= lo) | (hi <= win + COMBINE_WIN), axis=1)
    bounds = jnp.stack([lo, hi], axis=-1)
    i32 = lambda a: a.reshape(-1).astype(jnp.int32)
    nct = ctx_tokens // tm
    return pl.pallas_call(
        functools.partial(_combine_kernel, ctx_steps=nct),
        grid_spec=pltpu.PrefetchScalarGridSpec(
            num_scalar_prefetch=3,
            grid=(T // tm,),
            in_specs=[
                pl.BlockSpec((tm, N_EXPERTS), lambda i, *_: (i, 0)),
                pl.BlockSpec(memory_space=pl.ANY),
                pl.BlockSpec((tm, D_MODEL), lambda i, *_: (i, 0)),
                pl.BlockSpec((None, 1, D_MODEL), lambda i, *_: (i // tpg, 0, 5)),
                pl.BlockSpec((1, D_MODEL), lambda i, *_: (0, 0)),
            ],
            out_specs=[pl.BlockSpec((tm, D_MODEL), lambda i, *_: (jnp.minimum(i, nct - 1), 0)),
                       pl.BlockSpec((tm, D_MODEL), lambda i, *_: (jnp.maximum(i - nct, 0), 0))],
            scratch_shapes=[
                pltpu.VMEM((2, N_EXPERTS * COMBINE_WIN, D_MODEL), BF16),
                pltpu.VMEM((tm, N_EXPERTS * COMBINE_WIN), BF16),
                pltpu.VMEM((tm, D_MODEL), F32),
                pltpu.VMEM((GROUP_ROWS // 2, D_MODEL), BF16),
                pltpu.SemaphoreType.DMA((2,)),
                pltpu.SemaphoreType.DMA(()),
            ],
        ),
        out_shape=[jax.ShapeDtypeStruct((ctx_tokens, D_MODEL), F32),
                   jax.ShapeDtypeStruct((T - ctx_tokens, D_MODEL), F32)],
        compiler_params=_cparams(("arbitrary",), 48),
        name="combine",
    )(i32(win), i32(fast), i32(bounds), growt, ye, x1, mod3, g_post2)


def _rope_tables(n):
    rows = n // GRID_W
    row = jnp.repeat(jnp.arange(rows), GRID_W).astype(F32)
    col = jnp.tile(jnp.arange(GRID_W), rows).astype(F32)
    inv = ROPE_BASE ** (-jnp.arange(0, HD // 2, 2, dtype=F32) / (HD // 2))
    ang = jnp.concatenate([row[:, None] * inv, col[:, None] * inv], axis=-1)
    cos, sin = jnp.cos(ang), jnp.sin(ang)
    return jnp.tile(cos, (1, 4)), jnp.tile(jnp.concatenate([-sin, sin], axis=-1), (1, 2))


def kernel(x_prompt, x_sample, cache_k, cache_v, state_rnn, c, c_ctx, w_mod, b_mod, g_pre1, g_post1, g_pre2, g_post2, w_in, conv_w, conv_b, lru_wr, lru_br, lru_wi, lru_bi, lru_lam, lam_q1, lam_k1, lam_q2, lam_k2, g_sub, w_a_out, w_b_out, w_o, w_router, w_e_gate, w_e_up, w_e_down):
    nbc, nc, _ = x_prompt.shape
    nbl, nl, _ = x_sample.shape
    assert nbc * nc == GROUP_TOKENS and nl == GROUP_TOKENS and w_mod.shape[0] == 1
    tc = nbc * nc
    x_c, x_l = x_prompt.reshape(tc, D_MODEL), x_sample.reshape(nbl * nl, D_MODEL)
    sq0 = lambda a: a.reshape(a.shape[1:])
    sq1 = lambda a: a.reshape(a.shape[:1] + a.shape[2:])
    (w_mod, w_in, conv_w, lru_wr, lru_wi, lru_br, lru_bi, lru_lam, w_a_out, w_b_out, w_o, w_router, w_e_gate, w_e_up,
     w_e_down) = map(sq0, (w_mod, w_in, conv_w, lru_wr, lru_wi, lru_br, lru_bi, lru_lam, w_a_out, w_b_out, w_o,
                           w_router, w_e_gate, w_e_up, w_e_down))
    state_rnn, cache_k, cache_v = map(sq1, (state_rnn, cache_k, cache_v))

    cvec = jnp.concatenate([c_ctx[None, :], c, jnp.zeros((SUBLANES - 1 - nbl, D_MODEL), F32)], axis=0)
    mod = _modulation(cvec, w_mod, b_mod)
    mod3 = mod.reshape(SUBLANES, 1, 6 * D_MODEL)

    proj = _in_proj(x_c, x_l, mod3, g_pre1, w_in.astype(BF16))

    rnn_w = (conv_w, conv_b, lru_wr, lru_wi, lru_br, lru_bi, lru_lam)
    ya_c, h_fin = _rnn_branch(proj, *rnn_w, jnp.zeros((nbc, 2, D_RNN), F32), seq=nc, nseq=SUBLANES,
                              nb=nbc // SUBLANES, row0=0)
    ya_l, _ = _rnn_branch(proj, *rnn_w, state_rnn, seq=nl, nseq=1, nb=nbl, row0=tc // nl)

    lam_vecs = (lam_q1, lam_k1, lam_q2, lam_k2)
    on_c = _attention(proj, lam_vecs, g_sub, n=nc, nb=nbc, row0=0, tq=nc)
    past = cache_k.shape[1]
    on_l = _attention(proj, lam_vecs, g_sub, n=nl, nb=nbl, row0=tc // nl, tq=512,
                      cache=(cache_k.reshape(nbl, past, N_HEADS * VD), cache_v.reshape(nbl, past, N_HEADS * VD)),
                      tables=_rope_tables(nl))

    x1, h2, aff_t = _post_mix(ya_c, ya_l, on_c, on_l, proj, x_c, x_l, mod3, g_post1, g_pre2, w_a_out.astype(BF16),
                              w_b_out.astype(BF16), w_o.astype(BF16), w_router.T)

    capc = EC_FACTOR * nc // N_EXPERTS
    idx_c, val_c, growt_c = _route(aff_t, n=nc, nsets=nbc, col0=0, row_stride=capc)
    idx_l, val_l, growt_l = _route(aff_t, n=nl, nsets=nbl, col0=tc // nl, row_stride=0)
    gid_c = idx_c[..., 0] + (jnp.arange(nbc, dtype=jnp.int32) * nc)[:, None, None]
    gid_l = idx_l[..., 0] + (tc + jnp.arange(nbl, dtype=jnp.int32) * nl)[:, None, None]
    gidx = jnp.concatenate([gid_c.transpose(1, 0, 2).reshape(N_EXPERTS, -1),
                            gid_l.transpose(1, 0, 2).reshape(N_EXPERTS, -1)], axis=1)
    vals = jnp.concatenate([val_c.transpose(1, 0, 2, 3).reshape(N_EXPERTS, -1, 1),
                            val_l.transpose(1, 0, 2, 3).reshape(N_EXPERTS, -1, 1)], axis=1)
    growt = jnp.concatenate([growt_c, growt_l], axis=0)

    ye = _experts(gidx, h2, vals, w_e_gate, w_e_up, w_e_down)
    y_c, y_l = _combine(growt, ye, x1, mod3, g_post2, tc)

    y_prompt = y_c.reshape(nbc, nc, D_MODEL)
    y_sample = y_l.reshape(nbl, nl, D_MODEL)
    kblk = (2 * D_RNN + N_HEADS * VD) // (N_HEADS * VD)
    new_k = _take_cols(proj, tc, kblk, N_HEADS * VD).reshape(nbc, 1, nc, N_HEADS, 2 * HD)
    new_v = _take_cols(proj, tc, kblk + 1, N_HEADS * VD).reshape(nbc, 1, nc, N_HEADS, VD)
    return (y_prompt, y_sample, new_k, new_v, h_fin[:, None])
```
